```python
import jax, jax.numpy as jnp
from jax import lax
import numpy as np

D_MODEL = 2048
BATCH = 2
SEQ = 4096
DEPTH = 2
DEC_BATCH = 8
DEC_SEQ = 8
PAST_LEN = 16384
PAGE_SIZE = 128

N_HEADS = 8
HEAD_DIM = 128
ATT_WIDTH = N_HEADS * HEAD_DIM
CONV_WIDTH = D_MODEL - ATT_WIDTH
CONV_K = 31
D_FF = 4 * D_MODEL
PLE_DIM = 256
Q_BLOCK = 128
EPS = 1e-6
IN_WIDTH = 3 * ATT_WIDTH + N_HEADS + 2 * CONV_WIDTH
FORGET_BIAS_MIN = 3.0
FORGET_BIAS_MAX = 10.0

kernel_name = 'fox_conformer_hybrid_step'


def rms_norm(x, g):
    xf = x.astype(jnp.float32)
    y = xf * lax.rsqrt(jnp.mean(xf * xf, axis=-1, keepdims=True) + EPS)
    return (y * g.astype(jnp.float32)).astype(x.dtype)


def layer_norm(x, g, b):
    xf = x.astype(jnp.float32)
    mu = jnp.mean(xf, axis=-1, keepdims=True)
    xc = xf - mu
    y = xc * lax.rsqrt(jnp.mean(xc * xc, axis=-1, keepdims=True) + EPS)
    return (y * g.astype(jnp.float32) + b.astype(jnp.float32)).astype(x.dtype)


def mixer_inputs(x, g_mix, w_in, b_f):
    h = rms_norm(x, g_mix) @ w_in
    B, L = x.shape[0], x.shape[1]
    q, k, v, f_logit, c_in = jnp.split(
        h, [ATT_WIDTH, 2 * ATT_WIDTH, 3 * ATT_WIDTH, 3 * ATT_WIDTH + N_HEADS], axis=-1)
    q = q.reshape(B, L, N_HEADS, HEAD_DIM)
    k = k.reshape(B, L, N_HEADS, HEAD_DIM)
    v = v.reshape(B, L, N_HEADS, HEAD_DIM)
    logf = jax.nn.log_sigmoid((f_logit + b_f).astype(jnp.float32))
    u = c_in[..., :CONV_WIDTH] * jax.nn.sigmoid(c_in[..., CONV_WIDTH:])
    return q, k, v, logf, u


def fox_attention_prompt(q, k, v, logf):
    B, L = q.shape[0], q.shape[1]
    scale = HEAD_DIM ** -0.5
    c = jnp.cumsum(logf, axis=1).transpose(0, 2, 1)
    kpos = jnp.arange(L)

    def block(i):
        start = i * Q_BLOCK
        qs = lax.dynamic_slice_in_dim(q, start, Q_BLOCK, axis=1)
        cq = lax.dynamic_slice_in_dim(c, start, Q_BLOCK, axis=2)
        s = jnp.einsum('bqhd,bkhd->bhqk', qs, k).astype(jnp.float32) * scale
        s = s + cq[..., :, None] - c[..., None, :]
        qpos = start + jnp.arange(Q_BLOCK)
        s = jnp.where(kpos[None, :] <= qpos[:, None], s, -jnp.inf)
        p = jax.nn.softmax(s, axis=-1).astype(v.dtype)
        return jnp.einsum('bhqk,bkhd->bqhd', p, v)

    out = lax.map(block, jnp.arange(L // Q_BLOCK))
    return out.transpose(1, 0, 2, 3, 4).reshape(B, L, N_HEADS, HEAD_DIM)


def fox_attention_sample(q, k_new, v_new, logf_new, k_past, v_past, logf_past):
    T = q.shape[1]
    P = k_past.shape[1]
    scale = HEAD_DIM ** -0.5
    lp = logf_past.astype(jnp.float32)
    r_past = (lax.cumsum(lp, axis=1, reverse=True) - lp).transpose(0, 2, 1)
    cn = jnp.cumsum(logf_new, axis=1).transpose(0, 2, 1)
    s_past = jnp.einsum('bqhd,bkhd->bhqk', q, k_past).astype(jnp.float32) * scale
    s_past = s_past + cn[..., :, None] + r_past[..., None, :]
    s_new = jnp.einsum('bqhd,bkhd->bhqk', q, k_new).astype(jnp.float32) * scale
    s_new = s_new + cn[..., :, None] - cn[..., None, :]
    causal = jnp.arange(T)[None, :] <= jnp.arange(T)[:, None]
    s_new = jnp.where(causal, s_new, -jnp.inf)
    p = jax.nn.softmax(jnp.concatenate([s_past, s_new], axis=-1), axis=-1).astype(v_new.dtype)
    return (jnp.einsum('bhqk,bkhd->bqhd', p[..., :P], v_past)
            + jnp.einsum('bhqk,bkhd->bqhd', p[..., P:], v_new))


def conv_heads(u_ext, conv_w, conv_b, ln_g, ln_b):
    y = lax.conv_general_dilated(
        u_ext, conv_w[:, None, :], window_strides=(1,), padding='VALID',
        dimension_numbers=('NWC', 'WIO', 'NWC'), feature_group_count=CONV_WIDTH)
    y = layer_norm(y + conv_b, ln_g, ln_b)
    return jax.nn.silu(y)


def finish_layer(x, attn, conv_y, pe, g_attn_out, w_out, g_mlp, w_up, w_down, g_ple, w_ple, w_ple_gate):
    B, L = x.shape[0], x.shape[1]
    attn = rms_norm(attn, g_attn_out.reshape(N_HEADS, HEAD_DIM)).reshape(B, L, ATT_WIDTH)
    x = x + jnp.concatenate([attn, conv_y], axis=-1) @ w_out
    hid = jnp.square(jax.nn.relu(rms_norm(x, g_mlp) @ w_up))
    x = x + hid @ w_down
    gate = jax.nn.sigmoid(rms_norm(x, g_ple) @ w_ple_gate)
    return x + (pe @ w_ple) * gate


def setup_inputs(seed: int = 0) -> dict:
    key = jax.random.key(seed)
    ks = jax.random.split(key, 32)
    n_pages = PAST_LEN // PAGE_SIZE
    n_used = DEC_BATCH * n_pages
    n_phys = (5 * n_used + 3) // 4
    f32 = jnp.float32
    nrm = lambda k, shape, s: (jax.random.normal(k, shape, f32) * s).astype(f32)
    perm = jax.random.permutation(ks[0], n_phys)[:n_used]
    page_table = perm.reshape(DEC_BATCH, n_pages).astype(jnp.int32)
    bias_base = jnp.linspace(FORGET_BIAS_MIN, FORGET_BIAS_MAX, N_HEADS, dtype=f32)
    cache_logf = jax.nn.log_sigmoid(
        bias_base + 0.5 * jax.random.normal(ks[5], (DEPTH, n_phys, PAGE_SIZE, N_HEADS), f32))
    return {
        'x_prompt': nrm(ks[1], (BATCH, SEQ, D_MODEL), 1.0),
        'x_sample': nrm(ks[2], (DEC_BATCH, DEC_SEQ, D_MODEL), 1.0),
        'cache_k': nrm(ks[3], (DEPTH, n_phys, PAGE_SIZE, N_HEADS, HEAD_DIM), 1.0),
        'cache_v': nrm(ks[4], (DEPTH, n_phys, PAGE_SIZE, N_HEADS, HEAD_DIM), 1.0),
        'cache_logf': cache_logf.astype(f32),
        'state_conv': nrm(ks[6], (DEPTH, DEC_BATCH, CONV_K - 1, CONV_WIDTH), 0.5),
        'page_table': page_table,
        'p_prompt': nrm(ks[7], (DEPTH, BATCH, SEQ, PLE_DIM), 1.0),
        'p_sample': nrm(ks[8], (DEPTH, DEC_BATCH, DEC_SEQ, PLE_DIM), 1.0),
        'w_in': nrm(ks[9], (DEPTH, D_MODEL, IN_WIDTH), D_MODEL ** -0.5),
        'b_f': bias_base[None, :] + nrm(ks[10], (DEPTH, N_HEADS), 0.2),
        'conv_w': nrm(ks[11], (DEPTH, CONV_K, CONV_WIDTH), CONV_K ** -0.5),
        'conv_b': nrm(ks[12], (DEPTH, CONV_WIDTH), 0.01),
        'conv_ln_g': 1.0 + nrm(ks[13], (DEPTH, CONV_WIDTH), 0.1),
        'conv_ln_b': nrm(ks[14], (DEPTH, CONV_WIDTH), 0.01),
        'g_attn_out': 1.0 + nrm(ks[15], (DEPTH, ATT_WIDTH), 0.1),
        'w_out': nrm(ks[16], (DEPTH, D_MODEL, D_MODEL), D_MODEL ** -0.5),
        'g_mix': 1.0 + nrm(ks[17], (DEPTH, D_MODEL), 0.1),
        'g_mlp': 1.0 + nrm(ks[18], (DEPTH, D_MODEL), 0.1),
        'w_up': nrm(ks[19], (DEPTH, D_MODEL, D_FF), D_MODEL ** -0.5),
        'w_down': nrm(ks[20], (DEPTH, D_FF, D_MODEL), D_FF ** -0.5),
        'g_ple': 1.0 + nrm(ks[21], (DEPTH, D_MODEL), 0.1),
        'w_ple': nrm(ks[22], (DEPTH, PLE_DIM, D_MODEL), PLE_DIM ** -0.5),
        'w_ple_gate': nrm(ks[23], (DEPTH, D_MODEL, D_MODEL), D_MODEL ** -0.5),
        'g_final': 1.0 + nrm(ks[24], (D_MODEL,), 0.1),
    }


def reference(x_prompt, x_sample, cache_k, cache_v, cache_logf, state_conv, page_table,
              p_prompt, p_sample, w_in, b_f, conv_w, conv_b, conv_ln_g, conv_ln_b,
              g_attn_out, w_out, g_mix, g_mlp, w_up, w_down, g_ple, w_ple, w_ple_gate, g_final):
    dec_b, n_pages = page_table.shape
    page = cache_k.shape[2]
    past = n_pages * page
    xp, xs = x_prompt, x_sample
    kp, vp, lfp, cvp = [], [], [], []
    ksm, vsm, lfs, cvs = [], [], [], []
    for l in range(DEPTH):
        post = (g_attn_out[l], w_out[l], g_mlp[l], w_up[l], w_down[l], g_ple[l], w_ple[l], w_ple_gate[l])
        q, k, v, logf, u = mixer_inputs(xp, g_mix[l], w_in[l], b_f[l])
        attn = fox_attention_prompt(q, k, v, logf)
        u_ext = jnp.concatenate([jnp.zeros((u.shape[0], CONV_K - 1, CONV_WIDTH), u.dtype), u], axis=1)
        cy = conv_heads(u_ext, conv_w[l], conv_b[l], conv_ln_g[l], conv_ln_b[l])
        xp = finish_layer(xp, attn, cy, p_prompt[l], *post)
        kp.append(k); vp.append(v); lfp.append(logf); cvp.append(u_ext[:, -(CONV_K - 1):])
        q, k, v, logf, u = mixer_inputs(xs, g_mix[l], w_in[l], b_f[l])
        k_past = cache_k[l][page_table].reshape(dec_b, past, N_HEADS, HEAD_DIM)
        v_past = cache_v[l][page_table].reshape(dec_b, past, N_HEADS, HEAD_DIM)
        lf_past = cache_logf[l][page_table].reshape(dec_b, past, N_HEADS)
        attn = fox_attention_sample(q, k, v, logf, k_past, v_past, lf_past)
        u_ext = jnp.concatenate([state_conv[l].astype(u.dtype), u], axis=1)
        cy = conv_heads(u_ext, conv_w[l], conv_b[l], conv_ln_g[l], conv_ln_b[l])
        xs = finish_layer(xs, attn, cy, p_sample[l], *post)
        ksm.append(k); vsm.append(v); lfs.append(logf); cvs.append(u_ext[:, -(CONV_K - 1):])
    y_prompt = rms_norm(xp, g_final)
    y_sample = rms_norm(xs, g_final)
    return (y_prompt, y_sample,
            jnp.stack(kp), jnp.stack(vp), jnp.stack(lfp), jnp.stack(cvp),
            jnp.stack(ksm), jnp.stack(vsm), jnp.stack(lfs), jnp.stack(cvs))
```

```python
import functools

import jax
import jax.numpy as jnp
from jax import lax
from jax.experimental import pallas as pl
from jax.experimental.pallas import tpu as pltpu

D_MODEL = 2048
N_HEADS = 8
HEAD_DIM = 128
ATT_WIDTH = N_HEADS * HEAD_DIM
CONV_WIDTH = D_MODEL - ATT_WIDTH
CONV_K = 31
D_FF = 4 * D_MODEL
EPS = 1e-6

LANES = 128
CONV_HALO = 32
V7X_VMEM_BYTES = 64 * 1024 * 1024
VMEM_LIMIT = V7X_VMEM_BYTES - 8 * 1024 * 1024
NEG_INF = float("-inf")

f32 = jnp.float32
bf16 = jnp.bfloat16


def _dot(a, b):
    return jnp.dot(a, b, preferred_element_type=f32)


def _dot_exact(a, b):
    return jnp.dot(a, b, preferred_element_type=f32, precision=lax.Precision.HIGHEST)


def _rms(x, g):
    return x * lax.rsqrt(jnp.mean(x * x, axis=-1, keepdims=True) + EPS) * g


def _log_sigmoid(x):
    return jnp.minimum(x, 0.0) - jnp.log1p(jnp.exp(-jnp.abs(x)))


def _resident(shape):
    nd = len(shape)
    return pl.BlockSpec(shape, lambda *_: (0,) * nd, pipeline_mode=pl.Buffered(1))


def _params(sem):
    return pltpu.CompilerParams(dimension_semantics=sem, vmem_limit_bytes=VMEM_LIMIT)


def _in_proj_kernel(x_ref, g_ref, wqkv_ref, wf_ref, bf_ref, wcv_ref, wcg_ref,
                    q_ref, k_ref, v_ref, kb_ref, vb_ref, lf_ref, ccol_ref, crow_ref, u_ref,
                    carry_ref, *, tm, seg_len, nc):
    i = pl.program_id(0)
    xn = _rms(x_ref[...], g_ref[...]).astype(bf16)
    scale = HEAD_DIM ** -0.5
    for c in range(ATT_WIDTH // nc):
        sl = slice(c * nc, (c + 1) * nc)
        q_ref[:, sl] = (_dot(xn, wqkv_ref[:, sl]) * scale).astype(bf16)
        kk = _dot(xn, wqkv_ref[:, ATT_WIDTH + c * nc:ATT_WIDTH + (c + 1) * nc])
        k_ref[:, sl] = kk
        kb_ref[:, sl] = kk.astype(bf16)
        vv = _dot(xn, wqkv_ref[:, 2 * ATT_WIDTH + c * nc:2 * ATT_WIDTH + (c + 1) * nc])
        v_ref[:, sl] = vv
        vb_ref[:, sl] = vv.astype(bf16)
    for c in range(CONV_WIDTH // nc):
        sl = slice(c * nc, (c + 1) * nc)
        u_ref[:, sl] = _dot(xn, wcv_ref[:, sl]) * jax.nn.sigmoid(_dot(xn, wcg_ref[:, sl]))

    lf = _log_sigmoid(_dot(xn, wf_ref[...]) + bf_ref[...])
    lf_ref[...] = lf
    row = lax.broadcasted_iota(jnp.int32, (tm, tm), 0)
    col = lax.broadcasted_iota(jnp.int32, (tm, tm), 1)
    if seg_len >= tm:
        c = _dot_exact((row >= col).astype(f32), lf)

        @pl.when(lax.rem(i, seg_len // tm) == 0)
        def _():
            carry_ref[...] = jnp.zeros_like(carry_ref)

        c = c + carry_ref[...]
        carry_ref[...] = c[tm - 1:tm, :]
        crow_ref[...] = c.T[:N_HEADS, :]
    else:
        shift = seg_len.bit_length() - 1
        assert 1 << shift == seg_len
        same = (row >> shift) == (col >> shift)
        c = _dot_exact(((row >= col) & same).astype(f32), lf)
        crow_ref[...] = jnp.zeros_like(crow_ref)
    ccol_ref[...] = c


def _in_proj(x, g, wqkv, wf, bfp, wcv, wcg, *, tm, seg_len):
    m = x.shape[0]
    nc = 512
    row_f32 = lambda n: pl.BlockSpec((tm, n), lambda i: (i, 0))
    out_shape = (
        jax.ShapeDtypeStruct((m, ATT_WIDTH), bf16),
        jax.ShapeDtypeStruct((m, ATT_WIDTH), f32),
        jax.ShapeDtypeStruct((m, ATT_WIDTH), f32),
        jax.ShapeDtypeStruct((m, ATT_WIDTH), bf16),
        jax.ShapeDtypeStruct((m, ATT_WIDTH), bf16),
        jax.ShapeDtypeStruct((m, LANES), f32),
        jax.ShapeDtypeStruct((m, LANES), f32),
        jax.ShapeDtypeStruct((N_HEADS, m), f32),
        jax.ShapeDtypeStruct((m, CONV_WIDTH), f32),
    )
    out_specs = (row_f32(ATT_WIDTH), row_f32(ATT_WIDTH), row_f32(ATT_WIDTH), row_f32(ATT_WIDTH),
                 row_f32(ATT_WIDTH), row_f32(LANES), row_f32(LANES),
                 pl.BlockSpec((N_HEADS, tm), lambda i: (0, i)), row_f32(CONV_WIDTH))
    return pl.pallas_call(
        functools.partial(_in_proj_kernel, tm=tm, seg_len=seg_len, nc=nc),
        grid=(m // tm,),
        in_specs=[row_f32(D_MODEL), _resident((1, D_MODEL)), _resident(wqkv.shape), _resident(wf.shape),
                  _resident((1, LANES)), _resident(wcv.shape), _resident(wcg.shape)],
        out_specs=out_specs,
        out_shape=out_shape,
        scratch_shapes=[pltpu.VMEM((1, LANES), f32)],
        compiler_params=_params(("arbitrary",)),
        name="in_proj",
    )(x, g, wqkv, wf, bfp, wcv, wcg)


def _prompt_attn_kernel(q_ref, k_ref, v_ref, c_ref, g_ref, o_ref, *, tq):
    h = pl.program_id(1)
    i = pl.program_id(2)
    q = q_ref[...]

    def scores(j):
        start = pl.multiple_of(j * tq, tq)
        s = lax.dot_general(q, k_ref[pl.ds(start, tq), :], (((1,), (1,)), ((), ())),
                            preferred_element_type=f32)
        return s - c_ref[pl.ds(h, 1), pl.ds(start, tq)], start

    def update(s, start, carry):
        m, l, acc = carry
        m_new = jnp.maximum(m, jnp.max(s, axis=-1, keepdims=True))
        alpha = jnp.exp(m - m_new)
        p = jnp.exp(s - m_new)
        l = alpha * l + jnp.sum(p, axis=-1, keepdims=True)
        acc = alpha * acc + _dot(p.astype(bf16), v_ref[pl.ds(start, tq), :])
        return m_new, l, acc

    def body(j, carry):
        s, start = scores(j)
        return update(s, start, carry)

    init = (jnp.full((tq, 1), NEG_INF, f32), jnp.zeros((tq, 1), f32), jnp.zeros((tq, HEAD_DIM), f32))
    carry = lax.fori_loop(0, i, body, init)
    s, start = scores(i)
    row = lax.broadcasted_iota(jnp.int32, (tq, tq), 0)
    col = lax.broadcasted_iota(jnp.int32, (tq, tq), 1)
    s = jnp.where(col <= row, s, NEG_INF)
    _, l, acc = update(s, start, carry)
    o = acc / l
    o_ref[...] = _rms(o, g_ref[pl.ds(h, 1), :]).astype(bf16)


def _prompt_attn(q, kb, vb, crow, g_heads, *, batch, seq, tq):
    m = q.shape[0]
    nq = seq // tq
    return pl.pallas_call(
        functools.partial(_prompt_attn_kernel, tq=tq),
        grid=(batch, N_HEADS, nq),
        in_specs=[
            pl.BlockSpec((tq, HEAD_DIM), lambda b, h, i: (b * nq + i, h)),
            pl.BlockSpec((seq, HEAD_DIM), lambda b, h, i: (b, h)),
            pl.BlockSpec((seq, HEAD_DIM), lambda b, h, i: (b, h)),
            pl.BlockSpec((N_HEADS, seq), lambda b, h, i: (0, b)),
            _resident((N_HEADS, HEAD_DIM)),
        ],
        out_specs=pl.BlockSpec((tq, HEAD_DIM), lambda b, h, i: (b * nq + i, h)),
        out_shape=jax.ShapeDtypeStruct((m, ATT_WIDTH), bf16),
        compiler_params=_params(("arbitrary", "arbitrary", "arbitrary")),
        name="prompt_attn",
    )(q, kb, vb, crow, g_heads)


def _conv_kernel(u_ref, init_ref, w_ref, b_ref, g_ref, beta_ref, o_ref, ext_ref, *, tl, rc):
    j = pl.program_id(1)

    @pl.when(j == 0)
    def _():
        ext_ref[0:CONV_HALO, :] = init_ref[...]

    @pl.when(j > 0)
    def _():
        ext_ref[0:CONV_HALO, :] = ext_ref[tl:tl + CONV_HALO, :]

    ext_ref[CONV_HALO:CONV_HALO + tl, :] = u_ref[...]
    first = CONV_HALO - (CONV_K - 1)
    for r in range(tl // rc):
        acc = jnp.zeros((rc, CONV_WIDTH), f32)
        for t in range(CONV_K):
            acc = acc + w_ref[t:t + 1, :] * ext_ref[r * rc + first + t:r * rc + first + t + rc, :]
        y = acc + b_ref[...]
        yc = y - jnp.mean(y, axis=-1, keepdims=True)
        yn = yc * lax.rsqrt(jnp.mean(yc * yc, axis=-1, keepdims=True) + EPS) * g_ref[...] + beta_ref[...]
        o_ref[r * rc:(r + 1) * rc, :] = (yn * jax.nn.sigmoid(yn)).astype(bf16)


def _conv(u3, init, w, b, g, beta, *, tl, rc):
    nb, seq, _ = u3.shape
    return pl.pallas_call(
        functools.partial(_conv_kernel, tl=tl, rc=rc),
        grid=(nb, seq // tl),
        in_specs=[
            pl.BlockSpec((None, tl, CONV_WIDTH), lambda b_, j: (b_, j, 0)),
            pl.BlockSpec((None, CONV_HALO, CONV_WIDTH), lambda b_, j: (b_, 0, 0)),
            _resident((CONV_HALO, CONV_WIDTH)),
            _resident((1, CONV_WIDTH)), _resident((1, CONV_WIDTH)), _resident((1, CONV_WIDTH)),
        ],
        out_specs=pl.BlockSpec((None, tl, CONV_WIDTH), lambda b_, j: (b_, j, 0)),
        out_shape=jax.ShapeDtypeStruct((nb, seq, CONV_WIDTH), bf16),
        scratch_shapes=[pltpu.VMEM((CONV_HALO + tl, CONV_WIDTH), f32)],
        compiler_params=_params(("arbitrary", "arbitrary")),
        name="conv",
    )(u3, init, w, b, g, beta)


def _out_proj_kernel(x_ref, a_ref, cy_ref, w_ref, o_ref, *, nc):
    a = a_ref[...]
    cy = cy_ref[...]
    for c in range(D_MODEL // nc):
        sl = slice(c * nc, (c + 1) * nc)
        o_ref[:, sl] = (x_ref[:, sl] + _dot(a, w_ref[0:ATT_WIDTH, sl])
                        + _dot(cy, w_ref[ATT_WIDTH:D_MODEL, sl]))


def _out_proj(x, attn, cy, w, *, tm):
    m = x.shape[0]
    return pl.pallas_call(
        functools.partial(_out_proj_kernel, nc=512),
        grid=(m // tm,),
        in_specs=[pl.BlockSpec((tm, D_MODEL), lambda i: (i, 0)),
                  pl.BlockSpec((tm, ATT_WIDTH), lambda i: (i, 0)),
                  pl.BlockSpec((tm, CONV_WIDTH), lambda i: (i, 0)),
                  _resident(w.shape)],
        out_specs=pl.BlockSpec((tm, D_MODEL), lambda i: (i, 0)),
        out_shape=jax.ShapeDtypeStruct((m, D_MODEL), f32),
        compiler_params=_params(("arbitrary",)),
        name="out_proj",
    )(x, attn, cy, w)


def _mlp_kernel(x_ref, g_ref, wu_ref, wd_ref, o_ref, xn_ref):
    f = pl.program_id(1)

    @pl.when(f == 0)
    def _():
        xn_ref[...] = _rms(x_ref[...], g_ref[...]).astype(bf16)

    hid = jnp.square(jnp.maximum(_dot(xn_ref[...], wu_ref[...]), 0.0)).astype(bf16)
    part = _dot(hid, wd_ref[...])

    @pl.when(f == 0)
    def _():
        o_ref[...] = x_ref[...] + part

    @pl.when(f > 0)
    def _():
        o_ref[...] += part


def _mlp(x, g, wu, wd, *, tm, tf):
    m = x.shape[0]
    return pl.pallas_call(
        _mlp_kernel,
        grid=(m // tm, D_FF // tf),
        in_specs=[pl.BlockSpec((tm, D_MODEL), lambda i, f: (i, 0)),
                  _resident((1, D_MODEL)),
                  pl.BlockSpec((D_MODEL, tf), lambda i, f: (0, f)),
                  pl.BlockSpec((tf, D_MODEL), lambda i, f: (f, 0))],
        out_specs=pl.BlockSpec((tm, D_MODEL), lambda i, f: (i, 0)),
        out_shape=jax.ShapeDtypeStruct((m, D_MODEL), f32),
        scratch_shapes=[pltpu.VMEM((tm, D_MODEL), bf16)],
        compiler_params=_params(("arbitrary", "arbitrary")),
        name="mlp",
    )(x, g, wu, wd)


def _ple_kernel(x_ref, pe_ref, g_ref, wg_ref, wp_ref, gf_ref, o_ref, *, nc, final):
    xn = _rms(x_ref[...], g_ref[...]).astype(bf16)
    pe = pe_ref[...].astype(bf16)
    for c in range(D_MODEL // nc):
        sl = slice(c * nc, (c + 1) * nc)
        gate = jax.nn.sigmoid(_dot(xn, wg_ref[:, sl]))
        o_ref[:, sl] = x_ref[:, sl] + _dot(pe, wp_ref[:, sl]) * gate
    if final:
        o_ref[...] = _rms(o_ref[...], gf_ref[...])


def _ple(x, pe, g, wg, wp, gf, *, tm, final):
    m = x.shape[0]
    return pl.pallas_call(
        functools.partial(_ple_kernel, nc=512, final=final),
        grid=(m // tm,),
        in_specs=[pl.BlockSpec((tm, D_MODEL), lambda i: (i, 0)),
                  pl.BlockSpec((tm, pe.shape[1]), lambda i: (i, 0)),
                  _resident((1, D_MODEL)), _resident(wg.shape), _resident(wp.shape),
                  _resident((1, D_MODEL))],
        out_specs=pl.BlockSpec((tm, D_MODEL), lambda i: (i, 0)),
        out_shape=jax.ShapeDtypeStruct((m, D_MODEL), f32),
        compiler_params=_params(("arbitrary",)),
        name="ple",
    )(x, pe, g, wg, wp, gf)


def _decode_kernel(pt_ref, qt_ref, kn_ref, vn_ref, cn_ref, gt_ref, *rest, pages_per_step, n_tok):
    del pt_ref
    g_ = pages_per_step
    k_refs, v_refs, lf_refs = rest[0:g_], rest[g_:2 * g_], rest[2 * g_:3 * g_]
    o_ref, m_ref, l_ref, acc_ref, carry_ref = rest[3 * g_:]
    j = pl.program_id(1)
    page = k_refs[0].shape[0]
    qt = qt_ref[...]
    sub = lax.broadcasted_iota(jnp.int32, (N_HEADS, LANES), 0)
    lane = lax.broadcasted_iota(jnp.int32, (N_HEADS, LANES), 1)
    head_match = sub == (lane & (N_HEADS - 1))
    tile_heads = head_match.astype(f32)

    def online(s3, vp):
        n = s3.shape[0]
        m_old = m_ref[...]
        m_new = jnp.maximum(m_old, jnp.max(jnp.max(s3, axis=0), axis=0, keepdims=True))
        p3 = jnp.exp(s3 - m_new)
        alpha = jnp.exp(m_old - m_new)
        l_ref[...] = alpha * l_ref[...] + jnp.sum(jnp.sum(p3, axis=0), axis=0, keepdims=True)
        pb = p3.reshape(n * N_HEADS, LANES).astype(bf16)
        pv = lax.dot_general(vp, pb, (((0,), (0,)), ((), ())), preferred_element_type=f32)
        acc_ref[...] = alpha * acc_ref[...] + pv
        m_ref[...] = m_new

    @pl.when(j == 0)
    def _():
        m_ref[...] = jnp.full_like(m_ref, NEG_INF)
        l_ref[...] = jnp.zeros_like(l_ref)
        acc_ref[...] = jnp.zeros_like(acc_ref)
        carry_ref[...] = jnp.zeros_like(carry_ref)
        kn = kn_ref[...].astype(bf16)
        vn = vn_ref[...].astype(bf16)
        lanes2 = lax.broadcasted_iota(jnp.int32, (LANES, LANES), 0)
        cols2 = lax.broadcasted_iota(jnp.int32, (LANES, LANES), 1)
        tile128 = ((lanes2 < N_HEADS) & (lanes2 == (cols2 & (N_HEADS - 1)))).astype(f32)
        cn_t = _dot_exact(cn_ref[...], tile128)
        s3 = _dot(kn, qt).reshape(n_tok, N_HEADS, LANES) - cn_t[:, None, :]
        ktok = lax.broadcasted_iota(jnp.int32, (n_tok, N_HEADS, LANES), 0)
        qtok = lax.broadcasted_iota(jnp.int32, (n_tok, N_HEADS, LANES), 2) >> (N_HEADS.bit_length() - 1)
        s3 = jnp.where(head_match[None] & (ktok <= qtok), s3, NEG_INF)
        online(s3, vn)

    prow = lax.broadcasted_iota(jnp.int32, (page, page), 0)
    pcol = lax.broadcasted_iota(jnp.int32, (page, page), 1)
    suffix = (pcol > prow).astype(f32)
    for g in reversed(range(g_)):
        kp = k_refs[g][...].reshape(page * N_HEADS, HEAD_DIM).astype(bf16)
        vp = v_refs[g][...].reshape(page * N_HEADS, HEAD_DIM).astype(bf16)
        lpt = _dot_exact(lf_refs[g][...], tile_heads)
        carry = carry_ref[...]
        r = _dot_exact(suffix, lpt) + carry
        carry_ref[...] = carry + jnp.sum(lpt, axis=0, keepdims=True)
        s3 = _dot(kp, qt).reshape(page, N_HEADS, LANES) + r[:, None, :]
        s3 = jnp.where(head_match[None], s3, NEG_INF)
        online(s3, vp)

    @pl.when(j == pl.num_programs(1) - 1)
    def _():
        o = (acc_ref[...] / l_ref[...]).T
        o_ref[...] = _rms(o, gt_ref[...]).astype(bf16)


def _decode_attn(layer, page_table, qt, kn, vn, cn, gt, cache_k, cache_v, cache_logf, *, pages_per_step):
    nb, n_pages = page_table.shape
    page = cache_k.shape[2]
    g_ = pages_per_step
    n_groups = n_pages // g_
    n_tok = cn.shape[1]

    def page_map(g, nd):
        def index_map(b, j, pt):
            return (layer, pt[b * n_pages + (n_groups - 1 - j) * g_ + g]) + (0,) * nd
        return index_map

    kv_spec = lambda g: pl.BlockSpec((None, None, page, N_HEADS, HEAD_DIM), page_map(g, 3))
    lf_spec = lambda g: pl.BlockSpec((None, None, page, N_HEADS), page_map(g, 2))
    per_b = lambda shape: pl.BlockSpec((None,) + shape, lambda b, j, pt: (b,) + (0,) * len(shape))
    grid_spec = pltpu.PrefetchScalarGridSpec(
        num_scalar_prefetch=1,
        grid=(nb, n_groups),
        in_specs=[per_b((HEAD_DIM, LANES)), per_b((n_tok * N_HEADS, HEAD_DIM)),
                  per_b((n_tok * N_HEADS, HEAD_DIM)), per_b((n_tok, LANES)),
                  pl.BlockSpec((LANES, HEAD_DIM), lambda b, j, pt: (0, 0))]
                 + [kv_spec(g) for g in range(g_)] + [kv_spec(g) for g in range(g_)]
                 + [lf_spec(g) for g in range(g_)],
        out_specs=per_b((LANES, HEAD_DIM)),
        scratch_shapes=[pltpu.VMEM((1, LANES), f32), pltpu.VMEM((1, LANES), f32),
                        pltpu.VMEM((HEAD_DIM, LANES), f32), pltpu.VMEM((1, LANES), f32)],
    )
    return pl.pallas_call(
        functools.partial(_decode_kernel, pages_per_step=g_, n_tok=n_tok),
        grid_spec=grid_spec,
        out_shape=jax.ShapeDtypeStruct((nb, LANES, HEAD_DIM), bf16),
        compiler_params=_params(("arbitrary", "arbitrary")),
        name="decode_attn",
    )(page_table.reshape(-1), qt, kn, vn, cn, gt,
      *([cache_k] * g_), *([cache_v] * g_), *([cache_logf] * g_))


def _layer_weights(l, w_in, b_f, conv_w, conv_b, conv_ln_g, conv_ln_b, g_attn_out, w_out, g_mix, g_mlp,
                   w_up, w_down, g_ple, w_ple, w_ple_gate):
    a3 = 3 * ATT_WIDTH
    wl = w_in[l]
    row = lambda v: v.reshape(1, -1).astype(f32)
    g_heads = g_attn_out[l].reshape(N_HEADS, HEAD_DIM)
    return dict(
        wqkv=wl[:, :a3].astype(bf16),
        wf=jnp.pad(wl[:, a3:a3 + N_HEADS], ((0, 0), (0, LANES - N_HEADS))).astype(bf16),
        bf=jnp.pad(b_f[l], (0, LANES - N_HEADS)).reshape(1, LANES),
        wcv=wl[:, a3 + N_HEADS:a3 + N_HEADS + CONV_WIDTH].astype(bf16),
        wcg=wl[:, a3 + N_HEADS + CONV_WIDTH:].astype(bf16),
        conv_w=jnp.pad(conv_w[l], ((0, CONV_HALO - CONV_K), (0, 0))),
        conv_b=row(conv_b[l]), ln_g=row(conv_ln_g[l]), ln_b=row(conv_ln_b[l]),
        g_heads=g_heads, g_tiled=jnp.tile(g_heads, (LANES // N_HEADS, 1)),
        w_out=w_out[l].astype(bf16), g_mix=row(g_mix[l]), g_mlp=row(g_mlp[l]),
        w_up=w_up[l].astype(bf16), w_down=w_down[l].astype(bf16),
        g_ple=row(g_ple[l]), w_ple=w_ple[l].astype(bf16), w_gate=w_ple_gate[l].astype(bf16),
    )


def _finish(x, attn, cy, pe, w, gf, *, tm, tf, final):
    x = _out_proj(x, attn, cy, w["w_out"], tm=tm)
    x = _mlp(x, w["g_mlp"], w["w_up"], w["w_down"], tm=tm, tf=tf)
    return _ple(x, pe, w["g_ple"], w["w_gate"], w["w_ple"], gf, tm=tm, final=final)


def kernel(x_prompt, x_sample, cache_k, cache_v, cache_logf, state_conv, page_table, p_prompt, p_sample,
           w_in, b_f, conv_w, conv_b, conv_ln_g, conv_ln_b, g_attn_out, w_out, g_mix, g_mlp, w_up, w_down,
           g_ple, w_ple, w_ple_gate, g_final):
    depth = w_in.shape[0]
    nb, seq, _ = x_prompt.shape
    db, dseq, _ = x_sample.shape
    mp, ms = nb * seq, db * dseq
    keep = CONV_K - 1
    xp = x_prompt.reshape(mp, D_MODEL)
    xs = x_sample.reshape(ms, D_MODEL)
    gf = g_final.reshape(1, D_MODEL)
    zero_state = jnp.zeros((nb, CONV_HALO, CONV_WIDTH), f32)
    outs = {name: [] for name in ("kp", "vp", "lfp", "cvp", "ks", "vs", "lfs", "cvs")}
    for l in range(depth):
        w = _layer_weights(l, w_in, b_f, conv_w, conv_b, conv_ln_g, conv_ln_b, g_attn_out, w_out, g_mix,
                           g_mlp, w_up, w_down, g_ple, w_ple, w_ple_gate)
        final = l == depth - 1
        proj = (w["g_mix"], w["wqkv"], w["wf"], w["bf"], w["wcv"], w["wcg"])
        conv_p = (w["conv_w"], w["conv_b"], w["ln_g"], w["ln_b"])

        q, k, v, kb, vb, lf, _, crow, u = _in_proj(xp, *proj, tm=512, seg_len=seq)
        attn = _prompt_attn(q, kb, vb, crow, w["g_heads"], batch=nb, seq=seq, tq=512)
        u3 = u.reshape(nb, seq, CONV_WIDTH)
        cy = _conv(u3, zero_state, *conv_p, tl=256, rc=32).reshape(mp, CONV_WIDTH)
        xp = _finish(xp, attn, cy, p_prompt[l].reshape(mp, -1), w, gf, tm=512, tf=1024, final=final)
        outs["kp"].append(k.reshape(nb, seq, N_HEADS, HEAD_DIM))
        outs["vp"].append(v.reshape(nb, seq, N_HEADS, HEAD_DIM))
        outs["lfp"].append(lf[:, :N_HEADS].reshape(nb, seq, N_HEADS))
        outs["cvp"].append(u3[:, seq - keep:, :])

        q, k, v, _, _, lf, ccol, _, u = _in_proj(xs, *proj, tm=ms, seg_len=dseq)
        qt = jnp.pad(q.reshape(db, dseq * N_HEADS, HEAD_DIM).transpose(0, 2, 1),
                     ((0, 0), (0, 0), (0, LANES - dseq * N_HEADS)))
        attn = _decode_attn(l, page_table, qt, k.reshape(db, dseq * N_HEADS, HEAD_DIM),
                            v.reshape(db, dseq * N_HEADS, HEAD_DIM), ccol.reshape(db, dseq, LANES),
                            w["g_tiled"], cache_k, cache_v, cache_logf, pages_per_step=4)
        attn = attn[:, :dseq * N_HEADS, :].reshape(ms, ATT_WIDTH)
        u3 = u.reshape(db, dseq, CONV_WIDTH)
        state = jnp.pad(state_conv[l], ((0, 0), (CONV_HALO - keep, 0), (0, 0)))
        cy = _conv(u3, state, *conv_p, tl=dseq, rc=dseq).reshape(ms, CONV_WIDTH)
        xs = _finish(xs, attn, cy, p_sample[l].reshape(ms, -1), w, gf, tm=ms, tf=1024, final=final)
        outs["ks"].append(k.reshape(db, dseq, N_HEADS, HEAD_DIM))
        outs["vs"].append(v.reshape(db, dseq, N_HEADS, HEAD_DIM))
        outs["lfs"].append(lf[:, :N_HEADS].reshape(db, dseq, N_HEADS))
        outs["cvs"].append(jnp.concatenate([state_conv[l], u3], axis=1)[:, -keep:, :])
    st = lambda name: jnp.stack(outs[name])
    return (xp.reshape(nb, seq, D_MODEL), xs.reshape(db, dseq, D_MODEL),
            st("kp"), st("vp"), st("lfp"), st("cvp"), st("ks"), st("vs"), st("lfs"), st("cvs"))
```

```python
import functools

import jax
import jax.numpy as jnp
from jax import lax
from jax.experimental import pallas as pl
from jax.experimental.pallas import tpu as pltpu

D_MODEL = 2048
N_HEADS = 8
HEAD_DIM = 128
ATT_WIDTH = N_HEADS * HEAD_DIM
CONV_WIDTH = D_MODEL - ATT_WIDTH
CONV_K = 31
D_FF = 4 * D_MODEL
EPS = 1e-6

LANES = 128
SUBLANES = 8
BF16_ROWS = 16
CONV_HALO = 32
V7X_VMEM_BYTES = 64 * 1024 * 1024
VMEM_LIMIT = V7X_VMEM_BYTES - 8 * 1024 * 1024
NEG_INF = float("-inf")
NT_DIMS = (((1,), (1,)), ((), ()))

f32 = jnp.float32
bf16 = jnp.bfloat16


def _dot(a, b):
    return jnp.dot(a, b, preferred_element_type=f32)


def _dot_nt(a, b):
    return lax.dot_general(a, b, NT_DIMS, preferred_element_type=f32)


def _dot_exact(a, b):
    return jnp.dot(a, b, preferred_element_type=f32, precision=lax.Precision.HIGHEST)


def _rms(x, g):
    return x * lax.rsqrt(jnp.mean(x * x, axis=-1, keepdims=True) + EPS) * g


def _log_sigmoid(x):
    return jnp.minimum(x, 0.0) - jnp.log1p(jnp.exp(-jnp.abs(x)))


def _resident(shape):
    nd = len(shape)
    return pl.BlockSpec(shape, lambda *_: (0,) * nd, pipeline_mode=pl.Buffered(1))


def _resident_layer(arr, layer):
    nd = arr.ndim - 1
    return pl.BlockSpec((None,) + arr.shape[1:], lambda *_: (layer,) + (0,) * nd,
                        pipeline_mode=pl.Buffered(1))


def _params(sem):
    return pltpu.CompilerParams(dimension_semantics=sem, vmem_limit_bytes=VMEM_LIMIT)


def _in_proj_kernel(x_ref, g_ref, wqkv_ref, wf_ref, bf_ref, wcv_ref, wcg_ref, *rest, tm, seg_len, nc, prompt):
    if prompt:
        q_ref, k_ref, v_ref, kb_ref, vb_ref, lf_ref, crow_ref, u_ref, carry_ref = rest[-9:]
    else:
        q_ref, k_ref, v_ref, lf_ref, u_ref = rest
    i = pl.program_id(0)
    xn = _rms(x_ref[...], g_ref[...]).astype(bf16)
    scale = HEAD_DIM ** -0.5
    for c in range(ATT_WIDTH // nc):
        sl = slice(c * nc, (c + 1) * nc)
        q_ref[:, sl] = (_dot(xn, wqkv_ref[:, sl]) * scale).astype(bf16)
        kk = _dot(xn, wqkv_ref[:, ATT_WIDTH + c * nc:ATT_WIDTH + (c + 1) * nc])
        k_ref[:, sl] = kk
        vv = _dot(xn, wqkv_ref[:, 2 * ATT_WIDTH + c * nc:2 * ATT_WIDTH + (c + 1) * nc])
        v_ref[:, sl] = vv
        if prompt:
            kb_ref[:, sl] = kk.astype(bf16)
            vb_ref[:, sl] = vv.astype(bf16)
    for c in range(CONV_WIDTH // nc):
        sl = slice(c * nc, (c + 1) * nc)
        u_ref[:, sl] = _dot(xn, wcv_ref[:, sl]) * jax.nn.sigmoid(_dot(xn, wcg_ref[:, sl]))

    lf = _log_sigmoid(_dot(xn, wf_ref[...]) + bf_ref[...])
    if not prompt:
        lf_ref[...] = lf
        return
    row = lax.broadcasted_iota(jnp.int32, (tm, tm), 0)
    col = lax.broadcasted_iota(jnp.int32, (tm, tm), 1)
    c = _dot_exact((row >= col).astype(f32), lf)

    @pl.when(lax.rem(i, seg_len // tm) == 0)
    def _():
        carry_ref[...] = jnp.zeros_like(carry_ref)

    c = c + carry_ref[...]
    carry_ref[...] = c[tm - 1:tm, :]
    lf_ref[...] = lf.T[:N_HEADS, :]
    crow_ref[...] = c.T[:N_HEADS, :]


def _in_proj(layer, x, g, w, kv_prev, *, tm, seg_len, prompt):
    m = x.shape[0]
    depth = w["wqkv"].shape[0]
    nc = 512
    rows = lambda n: pl.BlockSpec((tm, n), lambda i: (i, 0))
    heads_rows = pl.BlockSpec((N_HEADS, tm), lambda i: (0, i))
    in_specs = [rows(D_MODEL), _resident_layer(g, layer), _resident_layer(w["wqkv"], layer),
                _resident_layer(w["wf"], layer), _resident_layer(w["bf"], layer),
                _resident_layer(w["wcv"], layer), _resident_layer(w["wcg"], layer)]
    args = [x, g, w["wqkv"], w["wf"], w["bf"], w["wcv"], w["wcg"]]
    aliases = {}
    if prompt:
        stacked = pl.BlockSpec((None, tm, ATT_WIDTH), lambda i: (layer, i, 0))
        out_shape = (
            jax.ShapeDtypeStruct((m, ATT_WIDTH), bf16),
            jax.ShapeDtypeStruct((depth, m, ATT_WIDTH), f32),
            jax.ShapeDtypeStruct((depth, m, ATT_WIDTH), f32),
            jax.ShapeDtypeStruct((m, ATT_WIDTH), bf16),
            jax.ShapeDtypeStruct((m, ATT_WIDTH), bf16),
            jax.ShapeDtypeStruct((N_HEADS, m), f32),
            jax.ShapeDtypeStruct((N_HEADS, m), f32),
            jax.ShapeDtypeStruct((m, CONV_WIDTH), f32),
        )
        out_specs = (rows(ATT_WIDTH), stacked, stacked, rows(ATT_WIDTH), rows(ATT_WIDTH),
                     heads_rows, heads_rows, rows(CONV_WIDTH))
        scratch = [pltpu.VMEM((1, LANES), f32)]
        if kv_prev is not None:
            in_specs += [pl.BlockSpec(memory_space=pl.ANY)] * 2
            args += list(kv_prev)
            aliases = {len(args) - 2: 1, len(args) - 1: 2}
    else:
        out_shape = (
            jax.ShapeDtypeStruct((m, ATT_WIDTH), bf16),
            jax.ShapeDtypeStruct((m, ATT_WIDTH), f32),
            jax.ShapeDtypeStruct((m, ATT_WIDTH), f32),
            jax.ShapeDtypeStruct((m, LANES), f32),
            jax.ShapeDtypeStruct((m, CONV_WIDTH), f32),
        )
        out_specs = (rows(ATT_WIDTH), rows(ATT_WIDTH), rows(ATT_WIDTH), rows(LANES), rows(CONV_WIDTH))
        scratch = []
    return pl.pallas_call(
        functools.partial(_in_proj_kernel, tm=tm, seg_len=seg_len, nc=nc, prompt=prompt),
        grid=(m // tm,),
        in_specs=in_specs,
        out_specs=out_specs,
        out_shape=out_shape,
        scratch_shapes=scratch,
        input_output_aliases=aliases,
        compiler_params=_params(("arbitrary",)),
        name="in_proj",
    )(*args)


def _prompt_attn_kernel(q_ref, k_ref, v_ref, c_ref, g_ref, o_ref, *, tq, hp):
    hg = pl.program_id(1)
    i = pl.program_id(2)
    lanes = [slice(e * HEAD_DIM, (e + 1) * HEAD_DIM) for e in range(hp)]

    def update(e, start, carry, mask):
        m, l, acc = carry
        s = (_dot_nt(q_ref[:, lanes[e]], k_ref[pl.ds(start, tq), lanes[e]])
             - c_ref[pl.ds(hg * hp + e, 1), pl.ds(start, tq)])
        if mask is not None:
            s = jnp.where(mask, s, NEG_INF)
        m_new = jnp.maximum(m, jnp.max(s, axis=-1, keepdims=True))
        alpha = jnp.exp(m - m_new)
        p = jnp.exp(s - m_new)
        l = alpha * l + jnp.sum(p, axis=-1, keepdims=True)
        acc = alpha * acc + _dot(p.astype(bf16), v_ref[pl.ds(start, tq), lanes[e]])
        return m_new, l, acc

    def body(j, carries):
        start = pl.multiple_of(j * tq, tq)
        return tuple(update(e, start, carries[e], None) for e in range(hp))

    init = (jnp.full((tq, 1), NEG_INF, f32), jnp.zeros((tq, 1), f32), jnp.zeros((tq, HEAD_DIM), f32))
    carries = lax.fori_loop(0, i, body, (init,) * hp)
    row = lax.broadcasted_iota(jnp.int32, (tq, tq), 0)
    col = lax.broadcasted_iota(jnp.int32, (tq, tq), 1)
    start = pl.multiple_of(i * tq, tq)
    for e in range(hp):
        _, l, acc = update(e, start, carries[e], col <= row)
        o_ref[:, lanes[e]] = _rms(acc / l, g_ref[pl.ds(hg * hp + e, 1), :]).astype(bf16)


def _prompt_attn(layer, q, kb, vb, crow, g_heads, *, batch, seq, tq, hp):
    m = q.shape[0]
    nq = seq // tq
    width = hp * HEAD_DIM
    return pl.pallas_call(
        functools.partial(_prompt_attn_kernel, tq=tq, hp=hp),
        grid=(batch, N_HEADS // hp, nq),
        in_specs=[
            pl.BlockSpec((tq, width), lambda b, h, i: (b * nq + i, h)),
            pl.BlockSpec((seq, width), lambda b, h, i: (b, h)),
            pl.BlockSpec((seq, width), lambda b, h, i: (b, h)),
            pl.BlockSpec((N_HEADS, seq), lambda b, h, i: (0, b)),
            _resident_layer(g_heads, layer),
        ],
        out_specs=pl.BlockSpec((tq, width), lambda b, h, i: (b * nq + i, h)),
        out_shape=jax.ShapeDtypeStruct((m, ATT_WIDTH), bf16),
        compiler_params=_params(("arbitrary", "arbitrary", "arbitrary")),
        name="prompt_attn",
    )(q, kb, vb, crow, g_heads)


def _conv_kernel(u_ref, init_ref, w_ref, b_ref, g_ref, beta_ref, o_ref, ext_ref, sh_ref, *, tl, rc):
    j = pl.program_id(1)

    @pl.when(j == 0)
    def _():
        ext_ref[0:CONV_HALO, :] = init_ref[...]

    @pl.when(j > 0)
    def _():
        ext_ref[0:CONV_HALO, :] = ext_ref[tl:tl + CONV_HALO, :]

    ext_ref[CONV_HALO:CONV_HALO + tl, :] = u_ref[...]
    n_sh = tl + CONV_HALO - SUBLANES
    for s in range(1, SUBLANES):
        sh_ref[s - 1] = ext_ref[s:s + n_sh, :]
    first = CONV_HALO - (CONV_K - 1)
    for r in range(tl // rc):
        acc = jnp.zeros((rc // SUBLANES, SUBLANES, CONV_WIDTH), f32)
        for t in range(CONV_K):
            a, s = divmod(first + t, SUBLANES)
            lo = r * rc + a * SUBLANES
            x = ext_ref[lo:lo + rc, :] if s == 0 else sh_ref[s - 1, lo:lo + rc, :]
            acc = acc + w_ref[t][None] * x.reshape(rc // SUBLANES, SUBLANES, CONV_WIDTH)
        y = acc.reshape(rc, CONV_WIDTH) + b_ref[...]
        yc = y - jnp.mean(y, axis=-1, keepdims=True)
        yn = yc * lax.rsqrt(jnp.mean(yc * yc, axis=-1, keepdims=True) + EPS) * g_ref[...] + beta_ref[...]
        o_ref[r * rc:(r + 1) * rc, :] = (yn * jax.nn.sigmoid(yn)).astype(bf16)


def _conv(layer, u3, init, w, *, tl, rc):
    nb, seq, _ = u3.shape
    return pl.pallas_call(
        functools.partial(_conv_kernel, tl=tl, rc=rc),
        grid=(nb, seq // tl),
        in_specs=[
            pl.BlockSpec((None, tl, CONV_WIDTH), lambda b_, j: (b_, j, 0)),
            pl.BlockSpec((None, CONV_HALO, CONV_WIDTH), lambda b_, j: (b_, 0, 0)),
            _resident_layer(w["conv_w"], layer), _resident_layer(w["conv_b"], layer),
            _resident_layer(w["ln_g"], layer), _resident_layer(w["ln_b"], layer),
        ],
        out_specs=pl.BlockSpec((None, tl, CONV_WIDTH), lambda b_, j: (b_, j, 0)),
        out_shape=jax.ShapeDtypeStruct((nb, seq, CONV_WIDTH), bf16),
        scratch_shapes=[pltpu.VMEM((CONV_HALO + tl, CONV_WIDTH), f32),
                        pltpu.VMEM((SUBLANES - 1, tl + CONV_HALO - SUBLANES, CONV_WIDTH), f32)],
        compiler_params=_params(("arbitrary", "arbitrary")),
        name="conv",
    )(u3, init, w["conv_w"], w["conv_b"], w["ln_g"], w["ln_b"])


def _out_proj_kernel(x_ref, a_ref, cy_ref, w_ref, o_ref, *, nc):
    a = a_ref[...]
    cy = cy_ref[...]
    for c in range(D_MODEL // nc):
        sl = slice(c * nc, (c + 1) * nc)
        o_ref[:, sl] = (x_ref[:, sl] + _dot(a, w_ref[0:ATT_WIDTH, sl])
                        + _dot(cy, w_ref[ATT_WIDTH:D_MODEL, sl]))


def _out_proj(layer, x, attn, cy, w_out, *, tm):
    m = x.shape[0]
    return pl.pallas_call(
        functools.partial(_out_proj_kernel, nc=512),
        grid=(m // tm,),
        in_specs=[pl.BlockSpec((tm, D_MODEL), lambda i: (i, 0)),
                  pl.BlockSpec((tm, ATT_WIDTH), lambda i: (i, 0)),
                  pl.BlockSpec((tm, CONV_WIDTH), lambda i: (i, 0)),
                  _resident_layer(w_out, layer)],
        out_specs=pl.BlockSpec((tm, D_MODEL), lambda i: (i, 0)),
        out_shape=jax.ShapeDtypeStruct((m, D_MODEL), f32),
        compiler_params=_params(("arbitrary",)),
        name="out_proj",
    )(x, attn, cy, w_out)


def _mlp_kernel(x_ref, g_ref, wu_ref, wd_ref, o_ref, xn_ref):
    f = pl.program_id(1)

    @pl.when(f == 0)
    def _():
        xn_ref[...] = _rms(x_ref[...], g_ref[...]).astype(bf16)

    hid = jnp.square(jnp.maximum(_dot(xn_ref[...], wu_ref[...]), 0.0)).astype(bf16)
    part = _dot(hid, wd_ref[...])

    @pl.when(f == 0)
    def _():
        o_ref[...] = x_ref[...] + part

    @pl.when(f > 0)
    def _():
        o_ref[...] += part


def _mlp(layer, x, g, wu, wd, *, tm, tf):
    m = x.shape[0]
    return pl.pallas_call(
        _mlp_kernel,
        grid=(m // tm, D_FF // tf),
        in_specs=[pl.BlockSpec((tm, D_MODEL), lambda i, f: (i, 0)),
                  _resident_layer(g, layer),
                  pl.BlockSpec((None, D_MODEL, tf), lambda i, f: (layer, 0, f)),
                  pl.BlockSpec((None, tf, D_MODEL), lambda i, f: (layer, f, 0))],
        out_specs=pl.BlockSpec((tm, D_MODEL), lambda i, f: (i, 0)),
        out_shape=jax.ShapeDtypeStruct((m, D_MODEL), f32),
        scratch_shapes=[pltpu.VMEM((tm, D_MODEL), bf16)],
        compiler_params=_params(("arbitrary", "arbitrary")),
        name="mlp",
    )(x, g, wu, wd)


def _ple_kernel(x_ref, pe_ref, g_ref, wg_ref, wp_ref, gf_ref, o_ref, *, nc, final):
    xn = _rms(x_ref[...], g_ref[...]).astype(bf16)
    pe = pe_ref[...].astype(bf16)
    for c in range(D_MODEL // nc):
        sl = slice(c * nc, (c + 1) * nc)
        gate = jax.nn.sigmoid(_dot(xn, wg_ref[:, sl]))
        o_ref[:, sl] = x_ref[:, sl] + _dot(pe, wp_ref[:, sl]) * gate
    if final:
        o_ref[...] = _rms(o_ref[...], gf_ref[...])


def _ple(layer, x, pe, g, wg, wp, gf, *, tm, final):
    m = x.shape[0]
    return pl.pallas_call(
        functools.partial(_ple_kernel, nc=512, final=final),
        grid=(m // tm,),
        in_specs=[pl.BlockSpec((tm, D_MODEL), lambda i: (i, 0)),
                  pl.BlockSpec((None, tm, pe.shape[2]), lambda i: (layer, i, 0)),
                  _resident_layer(g, layer), _resident_layer(wg, layer), _resident_layer(wp, layer),
                  _resident((1, D_MODEL))],
        out_specs=pl.BlockSpec((tm, D_MODEL), lambda i: (i, 0)),
        out_shape=jax.ShapeDtypeStruct((m, D_MODEL), f32),
        compiler_params=_params(("arbitrary",)),
        name="ple",
    )(x, pe, g, wg, wp, gf)


def _decode_kernel(pt_ref, q_ref, kn_ref, vn_ref, lfn_ref, gh_ref, *rest, pages_per_step, n_tok):
    del pt_ref
    g_ = pages_per_step
    k_refs, v_refs, lf_refs = rest[0:g_], rest[g_:2 * g_], rest[2 * g_:3 * g_]
    o_ref, m_ref, l_ref, acc_ref, carry_ref = rest[3 * g_:]
    j = pl.program_id(1)
    tp = q_ref.shape[1]
    page = lf_refs[0].shape[1]

    def online(s_heads, v_head):
        s = jnp.concatenate(s_heads, axis=0)
        m_old = m_ref[...]
        m_new = jnp.maximum(m_old, jnp.max(s, axis=-1, keepdims=True))
        p = jnp.exp(s - m_new)
        alpha = jnp.exp(m_old - m_new)
        l_ref[...] = alpha * l_ref[...] + jnp.sum(p, axis=-1, keepdims=True)
        pb = p.astype(bf16)
        pv = [_dot(pb[h * tp:(h + 1) * tp, :], v_head(h)) for h in range(N_HEADS)]
        acc_ref[...] = alpha * acc_ref[...] + jnp.concatenate(pv, axis=0)
        m_ref[...] = m_new

    @pl.when(j == 0)
    def _():
        m_ref[...] = jnp.full_like(m_ref, NEG_INF)
        l_ref[...] = jnp.zeros_like(l_ref)
        acc_ref[...] = jnp.zeros_like(acc_ref)
        carry_ref[...] = jnp.zeros_like(carry_ref)
        src = lax.broadcasted_iota(jnp.int32, (LANES, LANES), 0)
        dst = lax.broadcasted_iota(jnp.int32, (LANES, LANES), 1)
        cn = _dot_exact(lfn_ref[...], (src <= dst).astype(f32))
        tok = lax.broadcasted_iota(jnp.int32, (tp, LANES), 0)
        key = lax.broadcasted_iota(jnp.int32, (tp, LANES), 1)
        causal = (key <= tok) & (key < n_tok)
        s_heads = [jnp.where(causal, _dot_nt(q_ref[h], kn_ref[h]) - cn[h:h + 1, :], NEG_INF)
                   for h in range(N_HEADS)]
        online(s_heads, lambda h: vn_ref[h])

    src = lax.broadcasted_iota(jnp.int32, (page, 2 * LANES), 0)
    dst = lax.broadcasted_iota(jnp.int32, (page, 2 * LANES), 1)
    both = _dot_exact(jnp.concatenate([lf_refs[g][...] for g in range(g_)], axis=0),
                      ((src > dst) | (dst >= LANES)).astype(f32))
    run = carry_ref[...]
    r = [None] * g_
    for g in reversed(range(g_)):
        r[g] = both[g * N_HEADS:(g + 1) * N_HEADS, :LANES] + run
        run = run + both[g * N_HEADS:(g + 1) * N_HEADS, LANES:]
    carry_ref[...] = run

    def head_rows(refs, h):
        rows = [ref[pl.ds(h, page, stride=N_HEADS), :] for ref in refs]
        return jnp.concatenate(rows, axis=0).astype(bf16)

    k_head = functools.partial(head_rows, k_refs)
    v_head = functools.partial(head_rows, v_refs)

    s_heads = [_dot_nt(q_ref[h], k_head(h)) + jnp.concatenate([r[g][h:h + 1, :] for g in range(g_)], axis=1)
               for h in range(N_HEADS)]
    online(s_heads, v_head)

    @pl.when(j == pl.num_programs(1) - 1)
    def _():
        o = acc_ref[...] / l_ref[...]
        for h in range(N_HEADS):
            o_ref[h] = _rms(o[h * tp:(h + 1) * tp, :], gh_ref[h:h + 1, :]).astype(bf16)


def _decode_attn(layer, page_table, q, kn, vn, lfn, g_heads, cache_k, cache_v, cache_lf, *, pages_per_step,
                 n_tok):
    nb, n_pages = page_table.shape
    page = cache_lf.shape[3]
    g_ = pages_per_step
    n_groups = n_pages // g_
    tp = q.shape[2]

    def page_map(g, nd):
        def index_map(b, j, pt):
            return (layer, pt[b * n_pages + (n_groups - 1 - j) * g_ + g]) + (0,) * nd
        return index_map

    kv_spec = lambda g: pl.BlockSpec((None, None, page * N_HEADS, HEAD_DIM), page_map(g, 2))
    lf_spec = lambda g: pl.BlockSpec((None, None, N_HEADS, page), page_map(g, 2))
    per_b = lambda shape: pl.BlockSpec((None,) + shape, lambda b, j, pt: (b,) + (0,) * len(shape))
    grid_spec = pltpu.PrefetchScalarGridSpec(
        num_scalar_prefetch=1,
        grid=(nb, n_groups),
        in_specs=[per_b((N_HEADS, tp, HEAD_DIM)), per_b((N_HEADS, LANES, HEAD_DIM)),
                  per_b((N_HEADS, LANES, HEAD_DIM)), per_b((N_HEADS, LANES)),
                  pl.BlockSpec((None, N_HEADS, HEAD_DIM), lambda b, j, pt: (layer, 0, 0))]
                 + [kv_spec(g) for g in range(g_)] + [kv_spec(g) for g in range(g_)]
                 + [lf_spec(g) for g in range(g_)],
        out_specs=per_b((N_HEADS, tp, HEAD_DIM)),
        scratch_shapes=[pltpu.VMEM((N_HEADS * tp, 1), f32), pltpu.VMEM((N_HEADS * tp, 1), f32),
                        pltpu.VMEM((N_HEADS * tp, HEAD_DIM), f32), pltpu.VMEM((N_HEADS, LANES), f32)],
    )
    return pl.pallas_call(
        functools.partial(_decode_kernel, pages_per_step=g_, n_tok=n_tok),
        grid_spec=grid_spec,
        out_shape=jax.ShapeDtypeStruct((nb, N_HEADS, tp, HEAD_DIM), bf16),
        compiler_params=_params(("arbitrary", "arbitrary")),
        name="decode_attn",
    )(page_table.reshape(-1), q, kn, vn, lfn, g_heads,
      *([cache_k] * g_), *([cache_v] * g_), *([cache_lf] * g_))


def _stacked_weights(w_in, b_f, conv_w, conv_b, conv_ln_g, conv_ln_b, g_attn_out, w_out, g_mix, g_mlp,
                     w_up, w_down, g_ple, w_ple, w_ple_gate):
    a3 = 3 * ATT_WIDTH
    depth = w_in.shape[0]
    rows = lambda v: v.reshape(depth, 1, -1).astype(f32)
    return dict(
        wqkv=w_in[:, :, :a3].astype(bf16),
        wf=jnp.pad(w_in[:, :, a3:a3 + N_HEADS], ((0, 0), (0, 0), (0, LANES - N_HEADS))).astype(bf16),
        bf=jnp.pad(b_f, ((0, 0), (0, LANES - N_HEADS))).reshape(depth, 1, LANES),
        wcv=w_in[:, :, a3 + N_HEADS:a3 + N_HEADS + CONV_WIDTH].astype(bf16),
        wcg=w_in[:, :, a3 + N_HEADS + CONV_WIDTH:].astype(bf16),
        conv_w=jnp.broadcast_to(conv_w[:, :, None, :], (depth, CONV_K, SUBLANES, CONV_WIDTH)),
        conv_b=rows(conv_b), ln_g=rows(conv_ln_g), ln_b=rows(conv_ln_b),
        g_heads=g_attn_out.reshape(depth, N_HEADS, HEAD_DIM),
        w_out=w_out.astype(bf16), g_mix=rows(g_mix), g_mlp=rows(g_mlp),
        w_up=w_up.astype(bf16), w_down=w_down.astype(bf16),
        g_ple=rows(g_ple), w_ple=w_ple.astype(bf16), w_gate=w_ple_gate.astype(bf16),
    )


def _finish(layer, x, attn, cy, pe, w, gf, *, tm, tf, final):
    x = _out_proj(layer, x, attn, cy, w["w_out"], tm=tm)
    x = _mlp(layer, x, w["g_mlp"], w["w_up"], w["w_down"], tm=tm, tf=tf)
    return _ple(layer, x, pe, w["g_ple"], w["w_gate"], w["w_ple"], gf, tm=tm, final=final)


def kernel(x_prompt, x_sample, cache_k, cache_v, cache_logf, state_conv, page_table, p_prompt, p_sample,
           w_in, b_f, conv_w, conv_b, conv_ln_g, conv_ln_b, g_attn_out, w_out, g_mix, g_mlp, w_up, w_down,
           g_ple, w_ple, w_ple_gate, g_final):
    depth = w_in.shape[0]
    nb, seq, _ = x_prompt.shape
    db, dseq, _ = x_sample.shape
    mp, ms = nb * seq, db * dseq
    keep = CONV_K - 1
    xp = x_prompt.reshape(mp, D_MODEL)
    xs = x_sample.reshape(ms, D_MODEL)
    gf = g_final.reshape(1, D_MODEL)
    w = _stacked_weights(w_in, b_f, conv_w, conv_b, conv_ln_g, conv_ln_b, g_attn_out, w_out, g_mix, g_mlp,
                         w_up, w_down, g_ple, w_ple, w_ple_gate)
    pe_prompt = p_prompt.reshape(depth, mp, -1)
    pe_sample = p_sample.reshape(depth, ms, -1)
    cache_lf = jnp.swapaxes(cache_logf, 2, 3)
    cache_k = cache_k.reshape(cache_k.shape[0], cache_k.shape[1], -1, HEAD_DIM)
    cache_v = cache_v.reshape(cache_v.shape[0], cache_v.shape[1], -1, HEAD_DIM)
    zero_state = jnp.zeros((nb, CONV_HALO, CONV_WIDTH), f32)
    heads_first = lambda a: a.reshape(db, dseq, N_HEADS, -1).transpose(0, 2, 1, 3)
    pad_tok = lambda a, n: jnp.pad(a, ((0, 0), (0, 0), (0, n - dseq), (0, 0)))
    kv_prompt = None
    outs = {name: [] for name in ("lfp", "cvp", "ks", "vs", "lfs", "cvs")}
    for l in range(depth):
        final = l == depth - 1

        q, k_all, v_all, kb, vb, lf, crow, u = _in_proj(l, xp, w["g_mix"], w, kv_prompt, tm=512, seg_len=seq,
                                                        prompt=True)
        kv_prompt = (k_all, v_all)
        attn = _prompt_attn(l, q, kb, vb, crow, w["g_heads"], batch=nb, seq=seq, tq=512, hp=2)
        u3 = u.reshape(nb, seq, CONV_WIDTH)
        cy = _conv(l, u3, zero_state, w, tl=512, rc=32).reshape(mp, CONV_WIDTH)
        xp = _finish(l, xp, attn, cy, pe_prompt, w, gf, tm=512, tf=1024, final=final)
        outs["lfp"].append(lf.reshape(N_HEADS, nb, seq).transpose(1, 2, 0))
        outs["cvp"].append(u3[:, seq - keep:, :])

        q, k, v, lf, u = _in_proj(l, xs, w["g_mix"], w, None, tm=ms, seg_len=dseq, prompt=False)
        lf = lf[:, :N_HEADS].reshape(db, dseq, N_HEADS)
        attn = _decode_attn(
            l, page_table, pad_tok(heads_first(q), BF16_ROWS),
            pad_tok(heads_first(k), LANES).astype(bf16), pad_tok(heads_first(v), LANES).astype(bf16),
            jnp.pad(lf.transpose(0, 2, 1), ((0, 0), (0, 0), (0, LANES - dseq))),
            w["g_heads"], cache_k, cache_v, cache_lf, pages_per_step=8, n_tok=dseq)
        attn = attn[:, :, :dseq, :].transpose(0, 2, 1, 3).reshape(ms, ATT_WIDTH)
        u3 = u.reshape(db, dseq, CONV_WIDTH)
        state = jnp.pad(state_conv[l], ((0, 0), (CONV_HALO - keep, 0), (0, 0)))
        cy = _conv(l, u3, state, w, tl=dseq, rc=dseq).reshape(ms, CONV_WIDTH)
        xs = _finish(l, xs, attn, cy, pe_sample, w, gf, tm=ms, tf=1024, final=final)
        outs["ks"].append(k.reshape(db, dseq, N_HEADS, HEAD_DIM))
        outs["vs"].append(v.reshape(db, dseq, N_HEADS, HEAD_DIM))
        outs["lfs"].append(lf)
        outs["cvs"].append(jnp.concatenate([state_conv[l], u3], axis=1)[:, -keep:, :])
    st = lambda name: jnp.stack(outs[name])
    k_all, v_all = kv_prompt
    return (xp.reshape(nb, seq, D_MODEL), xs.reshape(db, dseq, D_MODEL),
            k_all.reshape(depth, nb, seq, N_HEADS, HEAD_DIM), v_all.reshape(depth, nb, seq, N_HEADS, HEAD_DIM),
            st("lfp"), st("cvp"), st("ks"), st("vs"), st("lfs"), st("cvs"))
```

```python
import functools

import jax
import jax.numpy as jnp
from jax import lax
from jax.experimental import pallas as pl
from jax.experimental.pallas import tpu as pltpu

D_MODEL = 2048
N_HEADS = 8
HEAD_DIM = 128
ATT_WIDTH = N_HEADS * HEAD_DIM
CONV_WIDTH = D_MODEL - ATT_WIDTH
CONV_K = 31
D_FF = 4 * D_MODEL
EPS = 1e-6

LANES = 128
SUBLANES = 8
BF16_ROWS = 16
CONV_HALO = 32
VT_ROWS = HEAD_DIM + BF16_ROWS
LOG2E = 1.4426950408889634
V7X_VMEM_BYTES = 64 * 1024 * 1024
VMEM_LIMIT = V7X_VMEM_BYTES - 8 * 1024 * 1024
NEG_INF = float("-inf")
NT_DIMS = (((1,), (1,)), ((), ()))

f32 = jnp.float32
bf16 = jnp.bfloat16


def _dot(a, b):
    return jnp.dot(a, b, preferred_element_type=f32)


def _dot_nt(a, b):
    return lax.dot_general(a, b, NT_DIMS, preferred_element_type=f32)


def _dot_exact(a, b):
    return jnp.dot(a, b, preferred_element_type=f32, precision=lax.Precision.HIGHEST)


def _rms(x, g):
    return x * lax.rsqrt(jnp.mean(x * x, axis=-1, keepdims=True) + EPS) * g


def _log_sigmoid(x):
    return jnp.minimum(x, 0.0) - jnp.log1p(jnp.exp(-jnp.abs(x)))


def _resident(shape):
    nd = len(shape)
    return pl.BlockSpec(shape, lambda *_: (0,) * nd, pipeline_mode=pl.Buffered(1))


def _resident_layer(arr, layer):
    nd = arr.ndim - 1
    return pl.BlockSpec((None,) + arr.shape[1:], lambda *_: (layer,) + (0,) * nd,
                        pipeline_mode=pl.Buffered(1))


def _params(sem):
    return pltpu.CompilerParams(dimension_semantics=sem, vmem_limit_bytes=VMEM_LIMIT)


def _in_proj_kernel(x_ref, g_ref, wqkv_ref, wf_ref, bf_ref, wcv_ref, wcg_ref, *rest, tm, seg_len, nc, prompt):
    if prompt:
        q_ref, k_ref, v_ref, kb_ref, vt_ref, lf_ref, crep_ref, u_ref, carry_ref = rest[-9:]
    else:
        q_ref, k_ref, v_ref, lf_ref, u_ref = rest
    i = pl.program_id(0)
    xn = _rms(x_ref[...], g_ref[...]).astype(bf16)
    scale = HEAD_DIM ** -0.5 * (LOG2E if prompt else 1.0)
    for c in range(ATT_WIDTH // nc):
        sl = slice(c * nc, (c + 1) * nc)
        q_ref[:, sl] = (_dot(xn, wqkv_ref[:, sl]) * scale).astype(bf16)
        kk = _dot(xn, wqkv_ref[:, ATT_WIDTH + c * nc:ATT_WIDTH + (c + 1) * nc])
        k_ref[:, sl] = kk
        vv = _dot(xn, wqkv_ref[:, 2 * ATT_WIDTH + c * nc:2 * ATT_WIDTH + (c + 1) * nc])
        v_ref[:, sl] = vv
        if prompt:
            kb_ref[:, sl] = kk.astype(bf16)
            for hh in range(nc // HEAD_DIM):
                head = c * (nc // HEAD_DIM) + hh
                vt_ref[head, 0:HEAD_DIM, :] = vv[:, hh * HEAD_DIM:(hh + 1) * HEAD_DIM].T.astype(bf16)
                vt_ref[head, HEAD_DIM:, :] = jnp.ones((BF16_ROWS, tm), bf16)
    for c in range(CONV_WIDTH // nc):
        sl = slice(c * nc, (c + 1) * nc)
        u_ref[:, sl] = _dot(xn, wcv_ref[:, sl]) * jax.nn.sigmoid(_dot(xn, wcg_ref[:, sl]))

    lf = _log_sigmoid(_dot(xn, wf_ref[...]) + bf_ref[...])
    if not prompt:
        lf_ref[...] = lf
        return
    row = lax.broadcasted_iota(jnp.int32, (tm, tm), 0)
    col = lax.broadcasted_iota(jnp.int32, (tm, tm), 1)
    c = _dot_exact((row >= col).astype(f32), lf)

    @pl.when(lax.rem(i, seg_len // tm) == 0)
    def _():
        carry_ref[...] = jnp.zeros_like(carry_ref)

    c = c + carry_ref[...]
    carry_ref[...] = c[tm - 1:tm, :]
    lf_ref[...] = lf.T[:N_HEADS, :]
    for h in range(N_HEADS):
        crep_ref[h] = jnp.broadcast_to(c[:, h:h + 1] * LOG2E, (tm, LANES))


def _in_proj(layer, x, g, w, kv_prev, *, tm, seg_len, prompt):
    m = x.shape[0]
    depth = w["wqkv"].shape[0]
    nc = 512
    rows = lambda n: pl.BlockSpec((tm, n), lambda i: (i, 0))
    heads_rows = pl.BlockSpec((N_HEADS, tm), lambda i: (0, i))
    in_specs = [rows(D_MODEL), _resident_layer(g, layer), _resident_layer(w["wqkv"], layer),
                _resident_layer(w["wf"], layer), _resident_layer(w["bf"], layer),
                _resident_layer(w["wcv"], layer), _resident_layer(w["wcg"], layer)]
    args = [x, g, w["wqkv"], w["wf"], w["bf"], w["wcv"], w["wcg"]]
    aliases = {}
    if prompt:
        stacked = pl.BlockSpec((None, tm, ATT_WIDTH), lambda i: (layer, i, 0))
        out_shape = (
            jax.ShapeDtypeStruct((m, ATT_WIDTH), bf16),
            jax.ShapeDtypeStruct((depth, m, ATT_WIDTH), f32),
            jax.ShapeDtypeStruct((depth, m, ATT_WIDTH), f32),
            jax.ShapeDtypeStruct((m, ATT_WIDTH), bf16),
            jax.ShapeDtypeStruct((N_HEADS, VT_ROWS, m), bf16),
            jax.ShapeDtypeStruct((N_HEADS, m), f32),
            jax.ShapeDtypeStruct((N_HEADS, m, LANES), f32),
            jax.ShapeDtypeStruct((m, CONV_WIDTH), f32),
        )
        out_specs = (rows(ATT_WIDTH), stacked, stacked, rows(ATT_WIDTH),
                     pl.BlockSpec((N_HEADS, VT_ROWS, tm), lambda i: (0, 0, i)), heads_rows,
                     pl.BlockSpec((N_HEADS, tm, LANES), lambda i: (0, i, 0)), rows(CONV_WIDTH))
        scratch = [pltpu.VMEM((1, LANES), f32)]
        if kv_prev is not None:
            in_specs += [pl.BlockSpec(memory_space=pl.ANY)] * 2
            args += list(kv_prev)
            aliases = {len(args) - 2: 1, len(args) - 1: 2}
    else:
        out_shape = (
            jax.ShapeDtypeStruct((m, ATT_WIDTH), bf16),
            jax.ShapeDtypeStruct((m, ATT_WIDTH), f32),
            jax.ShapeDtypeStruct((m, ATT_WIDTH), f32),
            jax.ShapeDtypeStruct((m, LANES), f32),
            jax.ShapeDtypeStruct((m, CONV_WIDTH), f32),
        )
        out_specs = (rows(ATT_WIDTH), rows(ATT_WIDTH), rows(ATT_WIDTH), rows(LANES), rows(CONV_WIDTH))
        scratch = []
    return pl.pallas_call(
        functools.partial(_in_proj_kernel, tm=tm, seg_len=seg_len, nc=nc, prompt=prompt),
        grid=(m // tm,),
        in_specs=in_specs,
        out_specs=out_specs,
        out_shape=out_shape,
        scratch_shapes=scratch,
        input_output_aliases=aliases,
        compiler_params=_params(("arbitrary",)),
        name="in_proj",
    )(*args)


def _prompt_attn_kernel(q_ref, k_ref, vt_ref, c_ref, g_ref, o_ref, *, tq, hp):
    i = pl.program_id(2)
    lanes = [slice(e * HEAD_DIM, (e + 1) * HEAD_DIM) for e in range(hp)]
    reps = tq // LANES

    def update(e, start, carry, mask):
        m, acc = carry
        bias = jnp.tile(c_ref[e, pl.ds(start, tq), :], (1, reps))
        s = _dot_nt(k_ref[pl.ds(start, tq), lanes[e]], q_ref[:, lanes[e]]) - bias
        if mask is not None:
            s = jnp.where(mask, s, NEG_INF)
        m_new = jnp.maximum(m, jnp.max(s, axis=0, keepdims=True))
        p = jnp.exp2(s - m_new)
        acc = jnp.exp2(m - m_new) * acc + _dot(vt_ref[e, :, pl.ds(start, tq)], p.astype(bf16))
        return m_new, acc

    def body(j, carries):
        start = pl.multiple_of(j * tq, tq)
        return tuple(update(e, start, carries[e], None) for e in range(hp))

    init = (jnp.full((1, tq), NEG_INF, f32), jnp.zeros((VT_ROWS, tq), f32))
    carries = lax.fori_loop(0, i, body, (init,) * hp)
    key = lax.broadcasted_iota(jnp.int32, (tq, tq), 0)
    qry = lax.broadcasted_iota(jnp.int32, (tq, tq), 1)
    start = pl.multiple_of(i * tq, tq)
    for e in range(hp):
        _, acc = update(e, start, carries[e], key <= qry)
        ot = acc[:HEAD_DIM] / acc[HEAD_DIM:HEAD_DIM + 1]
        ot = ot * lax.rsqrt(jnp.mean(ot * ot, axis=0, keepdims=True) + EPS) * jnp.tile(g_ref[e], (1, reps))
        o_ref[:, lanes[e]] = ot.T.astype(bf16)


def _prompt_attn(layer, q, kb, vt, crep, g_rep, *, batch, seq, tq, hp):
    m = q.shape[0]
    nq = seq // tq
    width = hp * HEAD_DIM
    return pl.pallas_call(
        functools.partial(_prompt_attn_kernel, tq=tq, hp=hp),
        grid=(batch, N_HEADS // hp, nq),
        in_specs=[
            pl.BlockSpec((tq, width), lambda b, h, i: (b * nq + i, h)),
            pl.BlockSpec((seq, width), lambda b, h, i: (b, h)),
            pl.BlockSpec((hp, VT_ROWS, seq), lambda b, h, i: (h, 0, b)),
            pl.BlockSpec((hp, seq, LANES), lambda b, h, i: (h, b, 0)),
            pl.BlockSpec((None, hp, HEAD_DIM, LANES), lambda b, h, i: (layer, h, 0, 0)),
        ],
        out_specs=pl.BlockSpec((tq, width), lambda b, h, i: (b * nq + i, h)),
        out_shape=jax.ShapeDtypeStruct((m, ATT_WIDTH), bf16),
        compiler_params=_params(("arbitrary", "arbitrary", "arbitrary")),
        name="prompt_attn",
    )(q, kb, vt, crep, g_rep)


def _conv_kernel(u_ref, init_ref, w_ref, b_ref, g_ref, beta_ref, o_ref, ext_ref, sh_ref, *, tl, rc):
    j = pl.program_id(1)

    @pl.when(j == 0)
    def _():
        ext_ref[0:CONV_HALO, :] = init_ref[...]

    @pl.when(j > 0)
    def _():
        ext_ref[0:CONV_HALO, :] = ext_ref[tl:tl + CONV_HALO, :]

    ext_ref[CONV_HALO:CONV_HALO + tl, :] = u_ref[...]
    n_sh = tl + CONV_HALO - SUBLANES
    for s in range(1, SUBLANES):
        sh_ref[s - 1] = ext_ref[s:s + n_sh, :]
    first = CONV_HALO - (CONV_K - 1)
    for r in range(tl // rc):
        acc = jnp.zeros((rc // SUBLANES, SUBLANES, CONV_WIDTH), f32)
        for t in range(CONV_K):
            a, s = divmod(first + t, SUBLANES)
            lo = r * rc + a * SUBLANES
            x = ext_ref[lo:lo + rc, :] if s == 0 else sh_ref[s - 1, lo:lo + rc, :]
            acc = acc + w_ref[t][None] * x.reshape(rc // SUBLANES, SUBLANES, CONV_WIDTH)
        y = acc.reshape(rc, CONV_WIDTH) + b_ref[...]
        yc = y - jnp.mean(y, axis=-1, keepdims=True)
        yn = yc * lax.rsqrt(jnp.mean(yc * yc, axis=-1, keepdims=True) + EPS) * g_ref[...] + beta_ref[...]
        o_ref[r * rc:(r + 1) * rc, :] = (yn * jax.nn.sigmoid(yn)).astype(bf16)


def _conv(layer, u3, init, w, *, tl, rc):
    nb, seq, _ = u3.shape
    return pl.pallas_call(
        functools.partial(_conv_kernel, tl=tl, rc=rc),
        grid=(nb, seq // tl),
        in_specs=[
            pl.BlockSpec((None, tl, CONV_WIDTH), lambda b_, j: (b_, j, 0)),
            pl.BlockSpec((None, CONV_HALO, CONV_WIDTH), lambda b_, j: (b_, 0, 0)),
            _resident_layer(w["conv_w"], layer), _resident_layer(w["conv_b"], layer),
            _resident_layer(w["ln_g"], layer), _resident_layer(w["ln_b"], layer),
        ],
        out_specs=pl.BlockSpec((None, tl, CONV_WIDTH), lambda b_, j: (b_, j, 0)),
        out_shape=jax.ShapeDtypeStruct((nb, seq, CONV_WIDTH), bf16),
        scratch_shapes=[pltpu.VMEM((CONV_HALO + tl, CONV_WIDTH), f32),
                        pltpu.VMEM((SUBLANES - 1, tl + CONV_HALO - SUBLANES, CONV_WIDTH), f32)],
        compiler_params=_params(("arbitrary", "arbitrary")),
        name="conv",
    )(u3, init, w["conv_w"], w["conv_b"], w["ln_g"], w["ln_b"])


def _out_proj_kernel(x_ref, a_ref, cy_ref, w_ref, o_ref, *, nc):
    a = a_ref[...]
    cy = cy_ref[...]
    for c in range(D_MODEL // nc):
        sl = slice(c * nc, (c + 1) * nc)
        o_ref[:, sl] = (x_ref[:, sl] + _dot(a, w_ref[0:ATT_WIDTH, sl])
                        + _dot(cy, w_ref[ATT_WIDTH:D_MODEL, sl]))


def _out_proj(layer, x, attn, cy, w_out, *, tm):
    m = x.shape[0]
    return pl.pallas_call(
        functools.partial(_out_proj_kernel, nc=512),
        grid=(m // tm,),
        in_specs=[pl.BlockSpec((tm, D_MODEL), lambda i: (i, 0)),
                  pl.BlockSpec((tm, ATT_WIDTH), lambda i: (i, 0)),
                  pl.BlockSpec((tm, CONV_WIDTH), lambda i: (i, 0)),
                  _resident_layer(w_out, layer)],
        out_specs=pl.BlockSpec((tm, D_MODEL), lambda i: (i, 0)),
        out_shape=jax.ShapeDtypeStruct((m, D_MODEL), f32),
        compiler_params=_params(("arbitrary",)),
        name="out_proj",
    )(x, attn, cy, w_out)


def _mlp_kernel(x_ref, g_ref, wu_ref, wd_ref, o_ref, xn_ref, *, nc):
    f = pl.program_id(1)

    @pl.when(f == 0)
    def _():
        x = x_ref[...]
        xn_ref[...] = _rms(x, g_ref[...]).astype(bf16)
        o_ref[...] = x

    hid = jnp.square(jnp.maximum(_dot(xn_ref[...], wu_ref[...]), 0.0)).astype(bf16)
    for c in range(D_MODEL // nc):
        sl = slice(c * nc, (c + 1) * nc)
        o_ref[:, sl] += _dot(hid, wd_ref[:, sl])


def _mlp(layer, x, g, wu, wd, *, tm, tf):
    m = x.shape[0]
    return pl.pallas_call(
        functools.partial(_mlp_kernel, nc=512),
        grid=(m // tm, D_FF // tf),
        in_specs=[pl.BlockSpec((tm, D_MODEL), lambda i, f: (i, 0)),
                  _resident_layer(g, layer),
                  pl.BlockSpec((None, D_MODEL, tf), lambda i, f: (layer, 0, f)),
                  pl.BlockSpec((None, tf, D_MODEL), lambda i, f: (layer, f, 0))],
        out_specs=pl.BlockSpec((tm, D_MODEL), lambda i, f: (i, 0)),
        out_shape=jax.ShapeDtypeStruct((m, D_MODEL), f32),
        scratch_shapes=[pltpu.VMEM((tm, D_MODEL), bf16)],
        compiler_params=_params(("arbitrary", "arbitrary")),
        name="mlp",
    )(x, g, wu, wd)


def _ple_kernel(x_ref, pe_ref, g_ref, wg_ref, wp_ref, gf_ref, o_ref, *, nc, final):
    xn = _rms(x_ref[...], g_ref[...]).astype(bf16)
    pe = pe_ref[...].astype(bf16)
    for c in range(D_MODEL // nc):
        sl = slice(c * nc, (c + 1) * nc)
        gate = jax.nn.sigmoid(_dot(xn, wg_ref[:, sl]))
        o_ref[:, sl] = x_ref[:, sl] + _dot(pe, wp_ref[:, sl]) * gate
    if final:
        o_ref[...] = _rms(o_ref[...], gf_ref[...])


def _ple(layer, x, pe, g, wg, wp, gf, *, tm, final):
    m = x.shape[0]
    return pl.pallas_call(
        functools.partial(_ple_kernel, nc=512, final=final),
        grid=(m // tm,),
        in_specs=[pl.BlockSpec((tm, D_MODEL), lambda i: (i, 0)),
                  pl.BlockSpec((None, tm, pe.shape[2]), lambda i: (layer, i, 0)),
                  _resident_layer(g, layer), _resident_layer(wg, layer), _resident_layer(wp, layer),
                  _resident((1, D_MODEL))],
        out_specs=pl.BlockSpec((tm, D_MODEL), lambda i: (i, 0)),
        out_shape=jax.ShapeDtypeStruct((m, D_MODEL), f32),
        compiler_params=_params(("arbitrary",)),
        name="ple",
    )(x, pe, g, wg, wp, gf)


def _decode_kernel(pt_ref, q_ref, kn_ref, vn_ref, lfn_ref, gh_ref, *rest, pages_per_step, n_tok):
    del pt_ref
    g_ = pages_per_step
    k_refs, v_refs, lf_refs = rest[0:g_], rest[g_:2 * g_], rest[2 * g_:3 * g_]
    o_ref, m_ref, l_ref, acc_ref, carry_ref = rest[3 * g_:]
    j = pl.program_id(1)
    tp = q_ref.shape[1]
    page = lf_refs[0].shape[1]

    def online(s_heads, v_head):
        s = jnp.concatenate(s_heads, axis=0)
        m_old = m_ref[...]
        m_new = jnp.maximum(m_old, jnp.max(s, axis=-1, keepdims=True))
        p = jnp.exp(s - m_new)
        alpha = jnp.exp(m_old - m_new)
        l_ref[...] = alpha * l_ref[...] + jnp.sum(p, axis=-1, keepdims=True)
        pb = p.astype(bf16)
        pv = [_dot(pb[h * tp:(h + 1) * tp, :], v_head(h)) for h in range(N_HEADS)]
        acc_ref[...] = alpha * acc_ref[...] + jnp.concatenate(pv, axis=0)
        m_ref[...] = m_new

    @pl.when(j == 0)
    def _():
        m_ref[...] = jnp.full_like(m_ref, NEG_INF)
        l_ref[...] = jnp.zeros_like(l_ref)
        acc_ref[...] = jnp.zeros_like(acc_ref)
        carry_ref[...] = jnp.zeros_like(carry_ref)
        src = lax.broadcasted_iota(jnp.int32, (LANES, LANES), 0)
        dst = lax.broadcasted_iota(jnp.int32, (LANES, LANES), 1)
        cn = _dot_exact(lfn_ref[...], (src <= dst).astype(f32))
        tok = lax.broadcasted_iota(jnp.int32, (tp, LANES), 0)
        key = lax.broadcasted_iota(jnp.int32, (tp, LANES), 1)
        causal = (key <= tok) & (key < n_tok)
        s_heads = [jnp.where(causal, _dot_nt(q_ref[h], kn_ref[h]) - cn[h:h + 1, :], NEG_INF)
                   for h in range(N_HEADS)]
        online(s_heads, lambda h: vn_ref[h])

    src = lax.broadcasted_iota(jnp.int32, (page, 2 * LANES), 0)
    dst = lax.broadcasted_iota(jnp.int32, (page, 2 * LANES), 1)
    both = _dot_exact(jnp.concatenate([lf_refs[g][...] for g in range(g_)], axis=0),
                      ((src > dst) | (dst >= LANES)).astype(f32))
    run = carry_ref[...]
    r = [None] * g_
    for g in reversed(range(g_)):
        r[g] = both[g * N_HEADS:(g + 1) * N_HEADS, :LANES] + run
        run = run + both[g * N_HEADS:(g + 1) * N_HEADS, LANES:]
    carry_ref[...] = run

    def head_rows(refs, h):
        rows = [ref[pl.ds(h, page, stride=N_HEADS), :] for ref in refs]
        return jnp.concatenate(rows, axis=0).astype(bf16)

    k_head = functools.partial(head_rows, k_refs)
    v_head = functools.partial(head_rows, v_refs)

    s_heads = [_dot_nt(q_ref[h], k_head(h)) + jnp.concatenate([r[g][h:h + 1, :] for g in range(g_)], axis=1)
               for h in range(N_HEADS)]
    online(s_heads, v_head)

    @pl.when(j == pl.num_programs(1) - 1)
    def _():
        o = acc_ref[...] / l_ref[...]
        for h in range(N_HEADS):
            o_ref[h] = _rms(o[h * tp:(h + 1) * tp, :], gh_ref[h:h + 1, :]).astype(bf16)


def _decode_attn(layer, page_table, q, kn, vn, lfn, g_heads, cache_k, cache_v, cache_lf, *, pages_per_step,
                 n_tok):
    nb, n_pages = page_table.shape
    page = cache_lf.shape[3]
    g_ = pages_per_step
    n_groups = n_pages // g_
    tp = q.shape[2]

    def page_map(g, nd):
        def index_map(b, j, pt):
            return (layer, pt[b * n_pages + (n_groups - 1 - j) * g_ + g]) + (0,) * nd
        return index_map

    kv_spec = lambda g: pl.BlockSpec((None, None, page * N_HEADS, HEAD_DIM), page_map(g, 2))
    lf_spec = lambda g: pl.BlockSpec((None, None, N_HEADS, page), page_map(g, 2))
    per_b = lambda shape: pl.BlockSpec((None,) + shape, lambda b, j, pt: (b,) + (0,) * len(shape))
    grid_spec = pltpu.PrefetchScalarGridSpec(
        num_scalar_prefetch=1,
        grid=(nb, n_groups),
        in_specs=[per_b((N_HEADS, tp, HEAD_DIM)), per_b((N_HEADS, LANES, HEAD_DIM)),
                  per_b((N_HEADS, LANES, HEAD_DIM)), per_b((N_HEADS, LANES)),
                  pl.BlockSpec((None, N_HEADS, HEAD_DIM), lambda b, j, pt: (layer, 0, 0))]
                 + [kv_spec(g) for g in range(g_)] + [kv_spec(g) for g in range(g_)]
                 + [lf_spec(g) for g in range(g_)],
        out_specs=per_b((N_HEADS, tp, HEAD_DIM)),
        scratch_shapes=[pltpu.VMEM((N_HEADS * tp, 1), f32), pltpu.VMEM((N_HEADS * tp, 1), f32),
                        pltpu.VMEM((N_HEADS * tp, HEAD_DIM), f32), pltpu.VMEM((N_HEADS, LANES), f32)],
    )
    return pl.pallas_call(
        functools.partial(_decode_kernel, pages_per_step=g_, n_tok=n_tok),
        grid_spec=grid_spec,
        out_shape=jax.ShapeDtypeStruct((nb, N_HEADS, tp, HEAD_DIM), bf16),
        compiler_params=_params(("arbitrary", "arbitrary")),
        name="decode_attn",
    )(page_table.reshape(-1), q, kn, vn, lfn, g_heads,
      *([cache_k] * g_), *([cache_v] * g_), *([cache_lf] * g_))


def _stacked_weights(w_in, b_f, conv_w, conv_b, conv_ln_g, conv_ln_b, g_attn_out, w_out, g_mix, g_mlp,
                     w_up, w_down, g_ple, w_ple, w_ple_gate):
    a3 = 3 * ATT_WIDTH
    depth = w_in.shape[0]
    rows = lambda v: v.reshape(depth, 1, -1).astype(f32)
    return dict(
        wqkv=w_in[:, :, :a3].astype(bf16),
        wf=jnp.pad(w_in[:, :, a3:a3 + N_HEADS], ((0, 0), (0, 0), (0, LANES - N_HEADS))).astype(bf16),
        bf=jnp.pad(b_f, ((0, 0), (0, LANES - N_HEADS))).reshape(depth, 1, LANES),
        wcv=w_in[:, :, a3 + N_HEADS:a3 + N_HEADS + CONV_WIDTH].astype(bf16),
        wcg=w_in[:, :, a3 + N_HEADS + CONV_WIDTH:].astype(bf16),
        conv_w=jnp.broadcast_to(conv_w[:, :, None, :], (depth, CONV_K, SUBLANES, CONV_WIDTH)),
        conv_b=rows(conv_b), ln_g=rows(conv_ln_g), ln_b=rows(conv_ln_b),
        g_heads=g_attn_out.reshape(depth, N_HEADS, HEAD_DIM),
        g_rep=jnp.broadcast_to(g_attn_out.reshape(depth, N_HEADS, HEAD_DIM, 1),
                               (depth, N_HEADS, HEAD_DIM, LANES)),
        w_out=w_out.astype(bf16), g_mix=rows(g_mix), g_mlp=rows(g_mlp),
        w_up=w_up.astype(bf16), w_down=w_down.astype(bf16),
        g_ple=rows(g_ple), w_ple=w_ple.astype(bf16), w_gate=w_ple_gate.astype(bf16),
    )


def _finish(layer, x, attn, cy, pe, w, gf, *, tm, tf, final):
    x = _out_proj(layer, x, attn, cy, w["w_out"], tm=tm)
    x = _mlp(layer, x, w["g_mlp"], w["w_up"], w["w_down"], tm=tm, tf=tf)
    return _ple(layer, x, pe, w["g_ple"], w["w_gate"], w["w_ple"], gf, tm=tm, final=final)


def kernel(x_prompt, x_sample, cache_k, cache_v, cache_logf, state_conv, page_table, p_prompt, p_sample,
           w_in, b_f, conv_w, conv_b, conv_ln_g, conv_ln_b, g_attn_out, w_out, g_mix, g_mlp, w_up, w_down,
           g_ple, w_ple, w_ple_gate, g_final):
    depth = w_in.shape[0]
    nb, seq, _ = x_prompt.shape
    db, dseq, _ = x_sample.shape
    mp, ms = nb * seq, db * dseq
    keep = CONV_K - 1
    xp = x_prompt.reshape(mp, D_MODEL)
    xs = x_sample.reshape(ms, D_MODEL)
    gf = g_final.reshape(1, D_MODEL)
    w = _stacked_weights(w_in, b_f, conv_w, conv_b, conv_ln_g, conv_ln_b, g_attn_out, w_out, g_mix, g_mlp,
                         w_up, w_down, g_ple, w_ple, w_ple_gate)
    pe_prompt = p_prompt.reshape(depth, mp, -1)
    pe_sample = p_sample.reshape(depth, ms, -1)
    cache_lf = jnp.swapaxes(cache_logf, 2, 3)
    cache_k = cache_k.reshape(cache_k.shape[0], cache_k.shape[1], -1, HEAD_DIM)
    cache_v = cache_v.reshape(cache_v.shape[0], cache_v.shape[1], -1, HEAD_DIM)
    zero_state = jnp.zeros((nb, CONV_HALO, CONV_WIDTH), f32)
    heads_first = lambda a: a.reshape(db, dseq, N_HEADS, -1).transpose(0, 2, 1, 3)
    pad_tok = lambda a, n: jnp.pad(a, ((0, 0), (0, 0), (0, n - dseq), (0, 0)))
    kv_prompt = None
    outs = {name: [] for name in ("lfp", "cvp", "ks", "vs", "lfs", "cvs")}
    for l in range(depth):
        final = l == depth - 1

        q, k_all, v_all, kb, vt, lf, crep, u = _in_proj(l, xp, w["g_mix"], w, kv_prompt, tm=512, seg_len=seq,
                                                        prompt=True)
        kv_prompt = (k_all, v_all)
        attn = _prompt_attn(l, q, kb, vt, crep, w["g_rep"], batch=nb, seq=seq, tq=1024, hp=2)
        u3 = u.reshape(nb, seq, CONV_WIDTH)
        cy = _conv(l, u3, zero_state, w, tl=512, rc=32).reshape(mp, CONV_WIDTH)
        xp = _finish(l, xp, attn, cy, pe_prompt, w, gf, tm=512, tf=1024, final=final)
        outs["lfp"].append(lf.reshape(N_HEADS, nb, seq).transpose(1, 2, 0))
        outs["cvp"].append(u3[:, seq - keep:, :])

        q, k, v, lf, u = _in_proj(l, xs, w["g_mix"], w, None, tm=ms, seg_len=dseq, prompt=False)
        lf = lf[:, :N_HEADS].reshape(db, dseq, N_HEADS)
        attn = _decode_attn(
            l, page_table, pad_tok(heads_first(q), BF16_ROWS),
            pad_tok(heads_first(k), LANES).astype(bf16), pad_tok(heads_first(v), LANES).astype(bf16),
            jnp.pad(lf.transpose(0, 2, 1), ((0, 0), (0, 0), (0, LANES - dseq))),
            w["g_heads"], cache_k, cache_v, cache_lf, pages_per_step=8, n_tok=dseq)
        attn = attn[:, :, :dseq, :].transpose(0, 2, 1, 3).reshape(ms, ATT_WIDTH)
        u3 = u.reshape(db, dseq, CONV_WIDTH)
        state = jnp.pad(state_conv[l], ((0, 0), (CONV_HALO - keep, 0), (0, 0)))
        cy = _conv(l, u3, state, w, tl=dseq, rc=dseq).reshape(ms, CONV_WIDTH)
        xs = _finish(l, xs, attn, cy, pe_sample, w, gf, tm=ms, tf=1024, final=final)
        outs["ks"].append(k.reshape(db, dseq, N_HEADS, HEAD_DIM))
        outs["vs"].append(v.reshape(db, dseq, N_HEADS, HEAD_DIM))
        outs["lfs"].append(lf)
        outs["cvs"].append(jnp.concatenate([state_conv[l], u3], axis=1)[:, -keep:, :])
    st = lambda name: jnp.stack(outs[name])
    k_all, v_all = kv_prompt
    return (xp.reshape(nb, seq, D_MODEL), xs.reshape(db, dseq, D_MODEL),
            k_all.reshape(depth, nb, seq, N_HEADS, HEAD_DIM), v_all.reshape(depth, nb, seq, N_HEADS, HEAD_DIM),
            st("lfp"), st("cvp"), st("ks"), st("vs"), st("lfs"), st("cvs"))
```

```python
import functools

import jax
import jax.numpy as jnp
from jax import lax
from jax.experimental import pallas as pl
from jax.experimental.pallas import tpu as pltpu

D_MODEL = 2048
N_HEADS = 8
HEAD_DIM = 128
ATT_WIDTH = N_HEADS * HEAD_DIM
CONV_WIDTH = D_MODEL - ATT_WIDTH
CONV_K = 31
D_FF = 4 * D_MODEL
EPS = 1e-6

LANES = 128
SUBLANES = 8
BF16_ROWS = 16
CONV_HALO = 32
VT_ROWS = HEAD_DIM + BF16_ROWS
LOG2E = 1.4426950408889634
V7X_VMEM_BYTES = 64 * 1024 * 1024
VMEM_LIMIT = V7X_VMEM_BYTES - 8 * 1024 * 1024
NEG_INF = float("-inf")
NT_DIMS = (((1,), (1,)), ((), ()))

f32 = jnp.float32
bf16 = jnp.bfloat16


def _dot(a, b):
    return jnp.dot(a, b, preferred_element_type=f32)


def _dot_nt(a, b):
    return lax.dot_general(a, b, NT_DIMS, preferred_element_type=f32)


def _dot_exact(a, b):
    return jnp.dot(a, b, preferred_element_type=f32, precision=lax.Precision.HIGHEST)


def _rms(x, g):
    return x * lax.rsqrt(jnp.mean(x * x, axis=-1, keepdims=True) + EPS) * g


def _log_sigmoid(x):
    return jnp.minimum(x, 0.0) - jnp.log1p(jnp.exp(-jnp.abs(x)))


def _resident(shape):
    nd = len(shape)
    return pl.BlockSpec(shape, lambda *_: (0,) * nd, pipeline_mode=pl.Buffered(1))


def _resident_layer(arr, layer):
    nd = arr.ndim - 1
    return pl.BlockSpec((None,) + arr.shape[1:], lambda *_: (layer,) + (0,) * nd,
                        pipeline_mode=pl.Buffered(1))


def _params(sem):
    return pltpu.CompilerParams(dimension_semantics=sem, vmem_limit_bytes=VMEM_LIMIT)


def _in_proj_kernel(x_ref, g_ref, wqkv_ref, wf_ref, bf_ref, wcv_ref, wcg_ref, *rest, tm, seg_len, nc, prompt):
    if prompt:
        q_ref, k_ref, v_ref, kb_ref, vt_ref, lf_ref, crep_ref, u_ref, carry_ref = rest[-9:]
    else:
        q_ref, k_ref, v_ref, lf_ref, u_ref = rest
    i = pl.program_id(0)
    xn = _rms(x_ref[...], g_ref[...]).astype(bf16)
    scale = HEAD_DIM ** -0.5 * (LOG2E if prompt else 1.0)
    for c in range(ATT_WIDTH // nc):
        sl = slice(c * nc, (c + 1) * nc)
        q_ref[:, sl] = (_dot(xn, wqkv_ref[:, sl]) * scale).astype(bf16)
        kk = _dot(xn, wqkv_ref[:, ATT_WIDTH + c * nc:ATT_WIDTH + (c + 1) * nc])
        k_ref[:, sl] = kk
        vv = _dot(xn, wqkv_ref[:, 2 * ATT_WIDTH + c * nc:2 * ATT_WIDTH + (c + 1) * nc])
        v_ref[:, sl] = vv
        if prompt:
            kb_ref[:, sl] = kk.astype(bf16)
            for hh in range(nc // HEAD_DIM):
                head = c * (nc // HEAD_DIM) + hh
                vt_ref[head, 0:HEAD_DIM, :] = vv[:, hh * HEAD_DIM:(hh + 1) * HEAD_DIM].T.astype(bf16)
                vt_ref[head, HEAD_DIM:, :] = jnp.ones((BF16_ROWS, tm), bf16)
    for c in range(CONV_WIDTH // nc):
        sl = slice(c * nc, (c + 1) * nc)
        u_ref[:, sl] = _dot(xn, wcv_ref[:, sl]) * jax.nn.sigmoid(_dot(xn, wcg_ref[:, sl]))

    lf = _log_sigmoid(_dot(xn, wf_ref[...]) + bf_ref[...])
    if not prompt:
        lf_ref[...] = lf
        return
    row = lax.broadcasted_iota(jnp.int32, (tm, tm), 0)
    col = lax.broadcasted_iota(jnp.int32, (tm, tm), 1)
    c = _dot_exact((row >= col).astype(f32), lf)

    @pl.when(lax.rem(i, seg_len // tm) == 0)
    def _():
        carry_ref[...] = jnp.zeros_like(carry_ref)

    c = c + carry_ref[...]
    carry_ref[...] = c[tm - 1:tm, :]
    lf_ref[...] = lf.T[:N_HEADS, :]
    for h in range(N_HEADS):
        crep_ref[h] = jnp.broadcast_to(c[:, h:h + 1] * LOG2E, (tm, LANES))


def _in_proj(layer, x, g, w, kv_prev, *, tm, seg_len, prompt):
    m = x.shape[0]
    depth = w["wqkv"].shape[0]
    nc = 512
    rows = lambda n: pl.BlockSpec((tm, n), lambda i: (i, 0))
    heads_rows = pl.BlockSpec((N_HEADS, tm), lambda i: (0, i))
    in_specs = [rows(D_MODEL), _resident_layer(g, layer), _resident_layer(w["wqkv"], layer),
                _resident_layer(w["wf"], layer), _resident_layer(w["bf"], layer),
                _resident_layer(w["wcv"], layer), _resident_layer(w["wcg"], layer)]
    args = [x, g, w["wqkv"], w["wf"], w["bf"], w["wcv"], w["wcg"]]
    aliases = {}
    if prompt:
        stacked = pl.BlockSpec((None, tm, ATT_WIDTH), lambda i: (layer, i, 0))
        out_shape = (
            jax.ShapeDtypeStruct((m, ATT_WIDTH), bf16),
            jax.ShapeDtypeStruct((depth, m, ATT_WIDTH), f32),
            jax.ShapeDtypeStruct((depth, m, ATT_WIDTH), f32),
            jax.ShapeDtypeStruct((m, ATT_WIDTH), bf16),
            jax.ShapeDtypeStruct((N_HEADS, VT_ROWS, m), bf16),
            jax.ShapeDtypeStruct((N_HEADS, m), f32),
            jax.ShapeDtypeStruct((N_HEADS, m, LANES), f32),
            jax.ShapeDtypeStruct((m, CONV_WIDTH), f32),
        )
        out_specs = (rows(ATT_WIDTH), stacked, stacked, rows(ATT_WIDTH),
                     pl.BlockSpec((N_HEADS, VT_ROWS, tm), lambda i: (0, 0, i)), heads_rows,
                     pl.BlockSpec((N_HEADS, tm, LANES), lambda i: (0, i, 0)), rows(CONV_WIDTH))
        scratch = [pltpu.VMEM((1, LANES), f32)]
        if kv_prev is not None:
            in_specs += [pl.BlockSpec(memory_space=pl.ANY)] * 2
            args += list(kv_prev)
            aliases = {len(args) - 2: 1, len(args) - 1: 2}
    else:
        out_shape = (
            jax.ShapeDtypeStruct((m, ATT_WIDTH), bf16),
            jax.ShapeDtypeStruct((m, ATT_WIDTH), f32),
            jax.ShapeDtypeStruct((m, ATT_WIDTH), f32),
            jax.ShapeDtypeStruct((m, LANES), f32),
            jax.ShapeDtypeStruct((m, CONV_WIDTH), f32),
        )
        out_specs = (rows(ATT_WIDTH), rows(ATT_WIDTH), rows(ATT_WIDTH), rows(LANES), rows(CONV_WIDTH))
        scratch = []
    return pl.pallas_call(
        functools.partial(_in_proj_kernel, tm=tm, seg_len=seg_len, nc=nc, prompt=prompt),
        grid=(m // tm,),
        in_specs=in_specs,
        out_specs=out_specs,
        out_shape=out_shape,
        scratch_shapes=scratch,
        input_output_aliases=aliases,
        compiler_params=_params(("arbitrary",)),
        name="in_proj",
    )(*args)


def _prompt_attn_kernel(q_ref, k_ref, vt_ref, c_ref, g_ref, o_ref, *, tq, hp):
    i = pl.program_id(2)
    lanes = [slice(e * HEAD_DIM, (e + 1) * HEAD_DIM) for e in range(hp)]
    reps = tq // LANES

    def update(e, start, carry, mask):
        m, acc = carry
        bias = jnp.tile(c_ref[e, pl.ds(start, tq), :], (1, reps))
        s = _dot_nt(k_ref[pl.ds(start, tq), lanes[e]], q_ref[:, lanes[e]]) - bias
        if mask is not None:
            s = jnp.where(mask, s, NEG_INF)
        m_new = jnp.maximum(m, jnp.max(s, axis=0, keepdims=True))
        p = jnp.exp2(s - m_new)
        acc = jnp.exp2(m - m_new) * acc + _dot(vt_ref[e, :, pl.ds(start, tq)], p.astype(bf16))
        return m_new, acc

    def body(j, carries):
        start = pl.multiple_of(j * tq, tq)
        return tuple(update(e, start, carries[e], None) for e in range(hp))

    init = (jnp.full((1, tq), NEG_INF, f32), jnp.zeros((VT_ROWS, tq), f32))
    carries = lax.fori_loop(0, i, body, (init,) * hp)
    key = lax.broadcasted_iota(jnp.int32, (tq, tq), 0)
    qry = lax.broadcasted_iota(jnp.int32, (tq, tq), 1)
    start = pl.multiple_of(i * tq, tq)
    for e in range(hp):
        _, acc = update(e, start, carries[e], key <= qry)
        ot = acc[:HEAD_DIM] / acc[HEAD_DIM:HEAD_DIM + 1]
        ot = ot * lax.rsqrt(jnp.mean(ot * ot, axis=0, keepdims=True) + EPS) * jnp.tile(g_ref[e], (1, reps))
        o_ref[:, lanes[e]] = ot.T.astype(bf16)


def _prompt_attn(layer, q, kb, vt, crep, g_rep, *, batch, seq, tq, hp):
    m = q.shape[0]
    nq = seq // tq
    width = hp * HEAD_DIM
    return pl.pallas_call(
        functools.partial(_prompt_attn_kernel, tq=tq, hp=hp),
        grid=(batch, N_HEADS // hp, nq),
        in_specs=[
            pl.BlockSpec((tq, width), lambda b, h, i: (b * nq + i, h)),
            pl.BlockSpec((seq, width), lambda b, h, i: (b, h)),
            pl.BlockSpec((hp, VT_ROWS, seq), lambda b, h, i: (h, 0, b)),
            pl.BlockSpec((hp, seq, LANES), lambda b, h, i: (h, b, 0)),
            pl.BlockSpec((None, hp, HEAD_DIM, LANES), lambda b, h, i: (layer, h, 0, 0)),
        ],
        out_specs=pl.BlockSpec((tq, width), lambda b, h, i: (b * nq + i, h)),
        out_shape=jax.ShapeDtypeStruct((m, ATT_WIDTH), bf16),
        compiler_params=_params(("arbitrary", "arbitrary", "arbitrary")),
        name="prompt_attn",
    )(q, kb, vt, crep, g_rep)


def _conv_kernel(u_ref, init_ref, w_ref, b_ref, g_ref, beta_ref, o_ref, ext_ref, sh_ref, *, tl, rc):
    j = pl.program_id(1)

    @pl.when(j == 0)
    def _():
        ext_ref[0:CONV_HALO, :] = init_ref[...]

    @pl.when(j > 0)
    def _():
        ext_ref[0:CONV_HALO, :] = ext_ref[tl:tl + CONV_HALO, :]

    ext_ref[CONV_HALO:CONV_HALO + tl, :] = u_ref[...]
    n_sh = tl + CONV_HALO - SUBLANES
    for s in range(1, SUBLANES):
        sh_ref[s - 1] = ext_ref[s:s + n_sh, :]
    first = CONV_HALO - (CONV_K - 1)
    for r in range(tl // rc):
        acc = jnp.zeros((rc // SUBLANES, SUBLANES, CONV_WIDTH), f32)
        for t in range(CONV_K):
            a, s = divmod(first + t, SUBLANES)
            lo = r * rc + a * SUBLANES
            x = ext_ref[lo:lo + rc, :] if s == 0 else sh_ref[s - 1, lo:lo + rc, :]
            acc = acc + w_ref[t][None] * x.reshape(rc // SUBLANES, SUBLANES, CONV_WIDTH)
        y = acc.reshape(rc, CONV_WIDTH) + b_ref[...]
        yc = y - jnp.mean(y, axis=-1, keepdims=True)
        yn = yc * lax.rsqrt(jnp.mean(yc * yc, axis=-1, keepdims=True) + EPS) * g_ref[...] + beta_ref[...]
        o_ref[r * rc:(r + 1) * rc, :] = (yn * jax.nn.sigmoid(yn)).astype(bf16)


def _conv(layer, u3, init, w, *, tl, rc):
    nb, seq, _ = u3.shape
    return pl.pallas_call(
        functools.partial(_conv_kernel, tl=tl, rc=rc),
        grid=(nb, seq // tl),
        in_specs=[
            pl.BlockSpec((None, tl, CONV_WIDTH), lambda b_, j: (b_, j, 0)),
            pl.BlockSpec((None, CONV_HALO, CONV_WIDTH), lambda b_, j: (b_, 0, 0)),
            _resident_layer(w["conv_w"], layer), _resident_layer(w["conv_b"], layer),
            _resident_layer(w["ln_g"], layer), _resident_layer(w["ln_b"], layer),
        ],
        out_specs=pl.BlockSpec((None, tl, CONV_WIDTH), lambda b_, j: (b_, j, 0)),
        out_shape=jax.ShapeDtypeStruct((nb, seq, CONV_WIDTH), bf16),
        scratch_shapes=[pltpu.VMEM((CONV_HALO + tl, CONV_WIDTH), f32),
                        pltpu.VMEM((SUBLANES - 1, tl + CONV_HALO - SUBLANES, CONV_WIDTH), f32)],
        compiler_params=_params(("arbitrary", "arbitrary")),
        name="conv",
    )(u3, init, w["conv_w"], w["conv_b"], w["ln_g"], w["ln_b"])


def _out_proj_kernel(x_ref, a_ref, cy_ref, w_ref, o_ref, *, nc):
    a = a_ref[...]
    cy = cy_ref[...]
    for c in range(D_MODEL // nc):
        sl = slice(c * nc, (c + 1) * nc)
        o_ref[:, sl] = (x_ref[:, sl] + _dot(a, w_ref[0:ATT_WIDTH, sl])
                        + _dot(cy, w_ref[ATT_WIDTH:D_MODEL, sl]))


def _out_proj(layer, x, attn, cy, w_out, *, tm):
    m = x.shape[0]
    return pl.pallas_call(
        functools.partial(_out_proj_kernel, nc=512),
        grid=(m // tm,),
        in_specs=[pl.BlockSpec((tm, D_MODEL), lambda i: (i, 0)),
                  pl.BlockSpec((tm, ATT_WIDTH), lambda i: (i, 0)),
                  pl.BlockSpec((tm, CONV_WIDTH), lambda i: (i, 0)),
                  _resident_layer(w_out, layer)],
        out_specs=pl.BlockSpec((tm, D_MODEL), lambda i: (i, 0)),
        out_shape=jax.ShapeDtypeStruct((m, D_MODEL), f32),
        compiler_params=_params(("arbitrary",)),
        name="out_proj",
    )(x, attn, cy, w_out)


def _mlp_kernel(x_ref, g_ref, wu_ref, wd_ref, o_ref, xn_ref, *, nc):
    f = pl.program_id(1)

    @pl.when(f == 0)
    def _():
        x = x_ref[...]
        xn_ref[...] = _rms(x, g_ref[...]).astype(bf16)
        o_ref[...] = x

    hid = jnp.square(jnp.maximum(_dot(xn_ref[...], wu_ref[...]), 0.0)).astype(bf16)
    for c in range(D_MODEL // nc):
        sl = slice(c * nc, (c + 1) * nc)
        o_ref[:, sl] += _dot(hid, wd_ref[:, sl])


def _mlp(layer, x, g, wu, wd, *, tm, tf):
    m = x.shape[0]
    return pl.pallas_call(
        functools.partial(_mlp_kernel, nc=512),
        grid=(m // tm, D_FF // tf),
        in_specs=[pl.BlockSpec((tm, D_MODEL), lambda i, f: (i, 0)),
                  _resident_layer(g, layer),
                  pl.BlockSpec((None, D_MODEL, tf), lambda i, f: (layer, 0, f)),
                  pl.BlockSpec((None, tf, D_MODEL), lambda i, f: (layer, f, 0))],
        out_specs=pl.BlockSpec((tm, D_MODEL), lambda i, f: (i, 0)),
        out_shape=jax.ShapeDtypeStruct((m, D_MODEL), f32),
        scratch_shapes=[pltpu.VMEM((tm, D_MODEL), bf16)],
        compiler_params=_params(("arbitrary", "arbitrary")),
        name="mlp",
    )(x, g, wu, wd)


def _ple_kernel(x_ref, pe_ref, g_ref, wg_ref, wp_ref, gf_ref, o_ref, *, nc, final):
    xn = _rms(x_ref[...], g_ref[...]).astype(bf16)
    pe = pe_ref[...].astype(bf16)
    for c in range(D_MODEL // nc):
        sl = slice(c * nc, (c + 1) * nc)
        gate = jax.nn.sigmoid(_dot(xn, wg_ref[:, sl]))
        o_ref[:, sl] = x_ref[:, sl] + _dot(pe, wp_ref[:, sl]) * gate
    if final:
        o_ref[...] = _rms(o_ref[...], gf_ref[...])


def _ple(layer, x, pe, g, wg, wp, gf, *, tm, final):
    m = x.shape[0]
    return pl.pallas_call(
        functools.partial(_ple_kernel, nc=512, final=final),
        grid=(m // tm,),
        in_specs=[pl.BlockSpec((tm, D_MODEL), lambda i: (i, 0)),
                  pl.BlockSpec((None, tm, pe.shape[2]), lambda i: (layer, i, 0)),
                  _resident_layer(g, layer), _resident_layer(wg, layer), _resident_layer(wp, layer),
                  _resident((1, D_MODEL))],
        out_specs=pl.BlockSpec((tm, D_MODEL), lambda i: (i, 0)),
        out_shape=jax.ShapeDtypeStruct((m, D_MODEL), f32),
        compiler_params=_params(("arbitrary",)),
        name="ple",
    )(x, pe, g, wg, wp, gf)


def _decode_kernel(pt_ref, q_ref, kn_ref, vn_ref, lfn_ref, gh_ref, *rest, pages_per_step, n_tok):
    del pt_ref
    g_ = pages_per_step
    k_refs, v_refs, lf_refs = rest[0:g_], rest[g_:2 * g_], rest[2 * g_:3 * g_]
    o_ref, m_ref, l_ref, acc_ref, carry_ref = rest[3 * g_:]
    j = pl.program_id(1)
    tp = q_ref.shape[1]
    page = lf_refs[0].shape[1]
    half = N_HEADS // 2
    order = [h for pair in range(half) for h in (pair, pair + half)]

    def online(s, pv_of_pair):
        m_old = m_ref[...]
        m_new = jnp.maximum(m_old, jnp.max(s, axis=-1, keepdims=True))
        p = jnp.exp(s - m_new)
        alpha = jnp.exp(m_old - m_new)
        l_ref[...] = alpha * l_ref[...] + jnp.sum(p, axis=-1, keepdims=True)
        pb = p.astype(bf16)
        pv = [pv_of_pair(pair, pb[2 * pair * tp:2 * (pair + 1) * tp, :]) for pair in range(half)]
        acc_ref[...] = alpha * acc_ref[...] + jnp.concatenate(pv, axis=0)
        m_ref[...] = m_new

    @pl.when(j == 0)
    def _():
        m_ref[...] = jnp.full_like(m_ref, NEG_INF)
        l_ref[...] = jnp.zeros_like(l_ref)
        acc_ref[...] = jnp.zeros_like(acc_ref)
        carry_ref[...] = jnp.zeros_like(carry_ref)
        src = lax.broadcasted_iota(jnp.int32, (LANES, LANES), 0)
        dst = lax.broadcasted_iota(jnp.int32, (LANES, LANES), 1)
        cn = _dot_exact(lfn_ref[...], (src <= dst).astype(f32))
        tok = lax.broadcasted_iota(jnp.int32, (tp, LANES), 0)
        key = lax.broadcasted_iota(jnp.int32, (tp, LANES), 1)
        causal = (key <= tok) & (key < n_tok)
        s_new = [jnp.where(causal, _dot_nt(q_ref[h], kn_ref[h]) - cn[h:h + 1, :], NEG_INF) for h in order]

        def pv_new(pair, pb):
            return jnp.concatenate([_dot(pb[0:tp, :], vn_ref[pair]), _dot(pb[tp:2 * tp, :], vn_ref[pair + half])],
                                   axis=0)

        online(jnp.concatenate(s_new, axis=0), pv_new)

    width = 2 * page
    src = lax.broadcasted_iota(jnp.int32, (page, 2 * width), 0)
    dst = lax.broadcasted_iota(jnp.int32, (page, 2 * width), 1)
    both = _dot_exact(jnp.concatenate([lf_refs[g][...] for g in range(g_)], axis=0),
                      ((src > (dst >> 1)) | (dst >= width)).astype(f32))
    run = carry_ref[...]
    r = [None] * g_
    for g in reversed(range(g_)):
        r[g] = both[g * N_HEADS:(g + 1) * N_HEADS, :width] + run
        run = run + both[g * N_HEADS:(g + 1) * N_HEADS, width:]
    carry_ref[...] = run

    def pair_rows(refs, pair):
        rows = [ref[pl.ds(pair, width, stride=half), :] for ref in refs]
        return jnp.concatenate(rows, axis=0).astype(bf16)

    row = lax.broadcasted_iota(jnp.int32, (2 * tp, g_ * width), 0)
    col = lax.broadcasted_iota(jnp.int32, (2 * tp, g_ * width), 1)
    own = (col & 1) == (row >= tp).astype(jnp.int32)
    s_pairs = []
    for pair in range(half):
        q2 = jnp.concatenate([q_ref[pair], q_ref[pair + half]], axis=0)
        bias = jnp.concatenate(
            [jnp.concatenate([jnp.broadcast_to(r[g][pair:pair + 1, :], (tp, width)),
                              jnp.broadcast_to(r[g][pair + half:pair + half + 1, :], (tp, width))], axis=0)
             for g in range(g_)], axis=1)
        s_pairs.append(jnp.where(own, _dot_nt(q2, pair_rows(k_refs, pair)) + bias, NEG_INF))
    online(jnp.concatenate(s_pairs, axis=0), lambda pair, pb: _dot(pb, pair_rows(v_refs, pair)))

    @pl.when(j == pl.num_programs(1) - 1)
    def _():
        o = acc_ref[...] / l_ref[...]
        for idx, h in enumerate(order):
            o_ref[h] = _rms(o[idx * tp:(idx + 1) * tp, :], gh_ref[h:h + 1, :]).astype(bf16)


def _decode_attn(layer, page_table, q, kn, vn, lfn, g_heads, cache_k, cache_v, cache_lf, *, pages_per_step,
                 n_tok):
    nb, n_pages = page_table.shape
    page = cache_lf.shape[3]
    g_ = pages_per_step
    n_groups = n_pages // g_
    tp = q.shape[2]

    def page_map(g, nd):
        def index_map(b, j, pt):
            return (layer, pt[b * n_pages + (n_groups - 1 - j) * g_ + g]) + (0,) * nd
        return index_map

    kv_spec = lambda g: pl.BlockSpec((None, None, page * N_HEADS, HEAD_DIM), page_map(g, 2))
    lf_spec = lambda g: pl.BlockSpec((None, None, N_HEADS, page), page_map(g, 2))
    per_b = lambda shape: pl.BlockSpec((None,) + shape, lambda b, j, pt: (b,) + (0,) * len(shape))
    grid_spec = pltpu.PrefetchScalarGridSpec(
        num_scalar_prefetch=1,
        grid=(nb, n_groups),
        in_specs=[per_b((N_HEADS, tp, HEAD_DIM)), per_b((N_HEADS, LANES, HEAD_DIM)),
                  per_b((N_HEADS, LANES, HEAD_DIM)), per_b((N_HEADS, LANES)),
                  pl.BlockSpec((None, N_HEADS, HEAD_DIM), lambda b, j, pt: (layer, 0, 0))]
                 + [kv_spec(g) for g in range(g_)] + [kv_spec(g) for g in range(g_)]
                 + [lf_spec(g) for g in range(g_)],
        out_specs=per_b((N_HEADS, tp, HEAD_DIM)),
        scratch_shapes=[pltpu.VMEM((N_HEADS * tp, 1), f32), pltpu.VMEM((N_HEADS * tp, 1), f32),
                        pltpu.VMEM((N_HEADS * tp, HEAD_DIM), f32), pltpu.VMEM((N_HEADS, 2 * page), f32)],
    )
    return pl.pallas_call(
        functools.partial(_decode_kernel, pages_per_step=g_, n_tok=n_tok),
        grid_spec=grid_spec,
        out_shape=jax.ShapeDtypeStruct((nb, N_HEADS, tp, HEAD_DIM), bf16),
        compiler_params=_params(("arbitrary", "arbitrary")),
        name="decode_attn",
    )(page_table.reshape(-1), q, kn, vn, lfn, g_heads,
      *([cache_k] * g_), *([cache_v] * g_), *([cache_lf] * g_))


def _stacked_weights(w_in, b_f, conv_w, conv_b, conv_ln_g, conv_ln_b, g_attn_out, w_out, g_mix, g_mlp,
                     w_up, w_down, g_ple, w_ple, w_ple_gate):
    a3 = 3 * ATT_WIDTH
    depth = w_in.shape[0]
    rows = lambda v: v.reshape(depth, 1, -1).astype(f32)
    return dict(
        wqkv=w_in[:, :, :a3].astype(bf16),
        wf=jnp.pad(w_in[:, :, a3:a3 + N_HEADS], ((0, 0), (0, 0), (0, LANES - N_HEADS))).astype(bf16),
        bf=jnp.pad(b_f, ((0, 0), (0, LANES - N_HEADS))).reshape(depth, 1, LANES),
        wcv=w_in[:, :, a3 + N_HEADS:a3 + N_HEADS + CONV_WIDTH].astype(bf16),
        wcg=w_in[:, :, a3 + N_HEADS + CONV_WIDTH:].astype(bf16),
        conv_w=jnp.broadcast_to(conv_w[:, :, None, :], (depth, CONV_K, SUBLANES, CONV_WIDTH)),
        conv_b=rows(conv_b), ln_g=rows(conv_ln_g), ln_b=rows(conv_ln_b),
        g_heads=g_attn_out.reshape(depth, N_HEADS, HEAD_DIM),
        g_rep=jnp.broadcast_to(g_attn_out.reshape(depth, N_HEADS, HEAD_DIM, 1),
                               (depth, N_HEADS, HEAD_DIM, LANES)),
        w_out=w_out.astype(bf16), g_mix=rows(g_mix), g_mlp=rows(g_mlp),
        w_up=w_up.astype(bf16), w_down=w_down.astype(bf16),
        g_ple=rows(g_ple), w_ple=w_ple.astype(bf16), w_gate=w_ple_gate.astype(bf16),
    )


def _finish(layer, x, attn, cy, pe, w, gf, *, tm, tf, final):
    x = _out_proj(layer, x, attn, cy, w["w_out"], tm=tm)
    x = _mlp(layer, x, w["g_mlp"], w["w_up"], w["w_down"], tm=tm, tf=tf)
    return _ple(layer, x, pe, w["g_ple"], w["w_gate"], w["w_ple"], gf, tm=tm, final=final)


def kernel(x_prompt, x_sample, cache_k, cache_v, cache_logf, state_conv, page_table, p_prompt, p_sample,
           w_in, b_f, conv_w, conv_b, conv_ln_g, conv_ln_b, g_attn_out, w_out, g_mix, g_mlp, w_up, w_down,
           g_ple, w_ple, w_ple_gate, g_final):
    depth = w_in.shape[0]
    nb, seq, _ = x_prompt.shape
    db, dseq, _ = x_sample.shape
    mp, ms = nb * seq, db * dseq
    keep = CONV_K - 1
    xp = x_prompt.reshape(mp, D_MODEL)
    xs = x_sample.reshape(ms, D_MODEL)
    gf = g_final.reshape(1, D_MODEL)
    w = _stacked_weights(w_in, b_f, conv_w, conv_b, conv_ln_g, conv_ln_b, g_attn_out, w_out, g_mix, g_mlp,
                         w_up, w_down, g_ple, w_ple, w_ple_gate)
    pe_prompt = p_prompt.reshape(depth, mp, -1)
    pe_sample = p_sample.reshape(depth, ms, -1)
    cache_lf = jnp.swapaxes(cache_logf, 2, 3)
    cache_k = cache_k.reshape(cache_k.shape[0], cache_k.shape[1], -1, HEAD_DIM)
    cache_v = cache_v.reshape(cache_v.shape[0], cache_v.shape[1], -1, HEAD_DIM)
    zero_state = jnp.zeros((nb, CONV_HALO, CONV_WIDTH), f32)
    heads_first = lambda a: a.reshape(db, dseq, N_HEADS, -1).transpose(0, 2, 1, 3)
    pad_tok = lambda a, n: jnp.pad(a, ((0, 0), (0, 0), (0, n - dseq), (0, 0)))
    kv_prompt = None
    outs = {name: [] for name in ("lfp", "cvp", "ks", "vs", "lfs", "cvs")}
    for l in range(depth):
        final = l == depth - 1

        q, k_all, v_all, kb, vt, lf, crep, u = _in_proj(l, xp, w["g_mix"], w, kv_prompt, tm=512, seg_len=seq,
                                                        prompt=True)
        kv_prompt = (k_all, v_all)
        attn = _prompt_attn(l, q, kb, vt, crep, w["g_rep"], batch=nb, seq=seq, tq=1024, hp=2)
        u3 = u.reshape(nb, seq, CONV_WIDTH)
        cy = _conv(l, u3, zero_state, w, tl=512, rc=32).reshape(mp, CONV_WIDTH)
        xp = _finish(l, xp, attn, cy, pe_prompt, w, gf, tm=512, tf=1024, final=final)
        outs["lfp"].append(lf.reshape(N_HEADS, nb, seq).transpose(1, 2, 0))
        outs["cvp"].append(u3[:, seq - keep:, :])

        q, k, v, lf, u = _in_proj(l, xs, w["g_mix"], w, None, tm=ms, seg_len=dseq, prompt=False)
        lf = lf[:, :N_HEADS].reshape(db, dseq, N_HEADS)
        attn = _decode_attn(
            l, page_table, pad_tok(heads_first(q), BF16_ROWS),
            pad_tok(heads_first(k), LANES).astype(bf16), pad_tok(heads_first(v), LANES).astype(bf16),
            jnp.pad(lf.transpose(0, 2, 1), ((0, 0), (0, 0), (0, LANES - dseq))),
            w["g_heads"], cache_k, cache_v, cache_lf, pages_per_step=8, n_tok=dseq)
        attn = attn[:, :, :dseq, :].transpose(0, 2, 1, 3).reshape(ms, ATT_WIDTH)
        u3 = u.reshape(db, dseq, CONV_WIDTH)
        state = jnp.pad(state_conv[l], ((0, 0), (CONV_HALO - keep, 0), (0, 0)))
        cy = _conv(l, u3, state, w, tl=dseq, rc=dseq).reshape(ms, CONV_WIDTH)
        xs = _finish(l, xs, attn, cy, pe_sample, w, gf, tm=ms, tf=1024, final=final)
        outs["ks"].append(k.reshape(db, dseq, N_HEADS, HEAD_DIM))
        outs["vs"].append(v.reshape(db, dseq, N_HEADS, HEAD_DIM))
        outs["lfs"].append(lf)
        outs["cvs"].append(jnp.concatenate([state_conv[l], u3], axis=1)[:, -keep:, :])
    st = lambda name: jnp.stack(outs[name])
    k_all, v_all = kv_prompt
    return (xp.reshape(nb, seq, D_MODEL), xs.reshape(db, dseq, D_MODEL),
            k_all.reshape(depth, nb, seq, N_HEADS, HEAD_DIM), v_all.reshape(depth, nb, seq, N_HEADS, HEAD_DIM),
            st("lfp"), st("cvp"), st("ks"), st("vs"), st("lfs"), st("cvs"))
```

```python
import functools

import jax
import jax.numpy as jnp
from jax import lax
from jax.experimental import pallas as pl
from jax.experimental.pallas import tpu as pltpu

D_MODEL = 2048
N_HEADS = 8
HEAD_DIM = 128
ATT_WIDTH = N_HEADS * HEAD_DIM
CONV_WIDTH = D_MODEL - ATT_WIDTH
CONV_K = 31
D_FF = 4 * D_MODEL
EPS = 1e-6

LANES = 128
SUBLANES = 8
BF16_ROWS = 16
CONV_HALO = 32
VT_ROWS = HEAD_DIM + BF16_ROWS
LOG2E = 1.4426950408889634
V7X_VMEM_BYTES = 64 * 1024 * 1024
VMEM_LIMIT = V7X_VMEM_BYTES - 8 * 1024 * 1024
NEG_INF = float("-inf")
NT_DIMS = (((1,), (1,)), ((), ()))

f32 = jnp.float32
bf16 = jnp.bfloat16


def _dot(a, b):
    return jnp.dot(a, b, preferred_element_type=f32)


def _dot_nt(a, b):
    return lax.dot_general(a, b, NT_DIMS, preferred_element_type=f32)


def _dot_exact(a, b):
    return jnp.dot(a, b, preferred_element_type=f32, precision=lax.Precision.HIGHEST)


def _rms(x, g):
    return x * lax.rsqrt(jnp.mean(x * x, axis=-1, keepdims=True) + EPS) * g


def _log_sigmoid(x):
    return jnp.minimum(x, 0.0) - jnp.log1p(jnp.exp(-jnp.abs(x)))


def _resident(shape):
    nd = len(shape)
    return pl.BlockSpec(shape, lambda *_: (0,) * nd, pipeline_mode=pl.Buffered(1))


def _resident_layer(arr, layer):
    nd = arr.ndim - 1
    return pl.BlockSpec((None,) + arr.shape[1:], lambda *_: (layer,) + (0,) * nd,
                        pipeline_mode=pl.Buffered(1))


def _params(sem):
    return pltpu.CompilerParams(dimension_semantics=sem, vmem_limit_bytes=VMEM_LIMIT)


def _in_proj_kernel(x_ref, g_ref, wqkv_ref, wf_ref, bf_ref, wcv_ref, wcg_ref, *rest, tm, seg_len, nc, prompt):
    if prompt:
        q_ref, k_ref, v_ref, kb_ref, vt_ref, lf_ref, crep_ref, u_ref, carry_ref = rest[-9:]
    else:
        q_ref, k_ref, v_ref, lf_ref, u_ref = rest
    if prompt:
        @pl.when(lax.rem(pl.program_id(0), seg_len // tm) == 0)
        def _():
            carry_ref[...] = jnp.zeros_like(carry_ref)

    xn = _rms(x_ref[...], g_ref[...]).astype(bf16)

    lf = _log_sigmoid(_dot(xn, wf_ref[...]) + bf_ref[...])
    if prompt:
        lane = lax.broadcasted_iota(jnp.int32, (tm, LANES), 1)
        hi = lf.astype(bf16).astype(f32)
        mid = (lf - hi).astype(bf16).astype(f32)
        lo = lf - hi - mid
        parts = jnp.where(lane < N_HEADS, hi,
                          jnp.where(lane < 2 * N_HEADS, mid, jnp.where(lane < 3 * N_HEADS, lo, 0.0)))
        row = lax.broadcasted_iota(jnp.int32, (tm, tm), 0)
        col = lax.broadcasted_iota(jnp.int32, (tm, tm), 1)
        cp = _dot((row >= col).astype(bf16), parts.astype(bf16)) + carry_ref[...]
        carry_ref[...] = cp[tm - 1:tm, :]
        lf_ref[...] = lf.T[:N_HEADS, :]
        for h in range(N_HEADS):
            c_h = cp[:, h:h + 1] + cp[:, h + N_HEADS:h + N_HEADS + 1] + cp[:, h + 2 * N_HEADS:h + 2 * N_HEADS + 1]
            crep_ref[h] = jnp.broadcast_to(c_h * LOG2E, (tm, LANES))
    else:
        lf_ref[...] = lf

    scale = HEAD_DIM ** -0.5 * (LOG2E if prompt else 1.0)
    for c in range(ATT_WIDTH // nc):
        sl = slice(c * nc, (c + 1) * nc)
        q_ref[:, sl] = (_dot(xn, wqkv_ref[:, sl]) * scale).astype(bf16)
        kk = _dot(xn, wqkv_ref[:, ATT_WIDTH + c * nc:ATT_WIDTH + (c + 1) * nc])
        k_ref[:, sl] = kk
        vv = _dot(xn, wqkv_ref[:, 2 * ATT_WIDTH + c * nc:2 * ATT_WIDTH + (c + 1) * nc])
        v_ref[:, sl] = vv
        if prompt:
            kb_ref[:, sl] = kk.astype(bf16)
            for hh in range(nc // HEAD_DIM):
                head = c * (nc // HEAD_DIM) + hh
                vt_ref[head, 0:HEAD_DIM, :] = vv[:, hh * HEAD_DIM:(hh + 1) * HEAD_DIM].T.astype(bf16)
                vt_ref[head, HEAD_DIM:, :] = jnp.ones((BF16_ROWS, tm), bf16)
    for c in range(CONV_WIDTH // nc):
        sl = slice(c * nc, (c + 1) * nc)
        u_ref[:, sl] = _dot(xn, wcv_ref[:, sl]) * jax.nn.sigmoid(_dot(xn, wcg_ref[:, sl]))


def _in_proj(layer, x, g, w, kv_prev, *, tm, seg_len, prompt):
    m = x.shape[0]
    depth = w["wqkv"].shape[0]
    nc = 512
    rows = lambda n: pl.BlockSpec((tm, n), lambda i: (i, 0))
    heads_rows = pl.BlockSpec((N_HEADS, tm), lambda i: (0, i))
    in_specs = [rows(D_MODEL), _resident_layer(g, layer), _resident_layer(w["wqkv"], layer),
                _resident_layer(w["wf"], layer), _resident_layer(w["bf"], layer),
                _resident_layer(w["wcv"], layer), _resident_layer(w["wcg"], layer)]
    args = [x, g, w["wqkv"], w["wf"], w["bf"], w["wcv"], w["wcg"]]
    aliases = {}
    if prompt:
        stacked = pl.BlockSpec((None, tm, ATT_WIDTH), lambda i: (layer, i, 0))
        out_shape = (
            jax.ShapeDtypeStruct((m, ATT_WIDTH), bf16),
            jax.ShapeDtypeStruct((depth, m, ATT_WIDTH), f32),
            jax.ShapeDtypeStruct((depth, m, ATT_WIDTH), f32),
            jax.ShapeDtypeStruct((m, ATT_WIDTH), bf16),
            jax.ShapeDtypeStruct((N_HEADS, VT_ROWS, m), bf16),
            jax.ShapeDtypeStruct((N_HEADS, m), f32),
            jax.ShapeDtypeStruct((N_HEADS, m, LANES), f32),
            jax.ShapeDtypeStruct((m, CONV_WIDTH), f32),
        )
        out_specs = (rows(ATT_WIDTH), stacked, stacked, rows(ATT_WIDTH),
                     pl.BlockSpec((N_HEADS, VT_ROWS, tm), lambda i: (0, 0, i)), heads_rows,
                     pl.BlockSpec((N_HEADS, tm, LANES), lambda i: (0, i, 0)), rows(CONV_WIDTH))
        scratch = [pltpu.VMEM((1, LANES), f32)]
        if kv_prev is not None:
            in_specs += [pl.BlockSpec(memory_space=pl.ANY)] * 2
            args += list(kv_prev)
            aliases = {len(args) - 2: 1, len(args) - 1: 2}
    else:
        out_shape = (
            jax.ShapeDtypeStruct((m, ATT_WIDTH), bf16),
            jax.ShapeDtypeStruct((m, ATT_WIDTH), f32),
            jax.ShapeDtypeStruct((m, ATT_WIDTH), f32),
            jax.ShapeDtypeStruct((m, LANES), f32),
            jax.ShapeDtypeStruct((m, CONV_WIDTH), f32),
        )
        out_specs = (rows(ATT_WIDTH), rows(ATT_WIDTH), rows(ATT_WIDTH), rows(LANES), rows(CONV_WIDTH))
        scratch = []
    return pl.pallas_call(
        functools.partial(_in_proj_kernel, tm=tm, seg_len=seg_len, nc=nc, prompt=prompt),
        grid=(m // tm,),
        in_specs=in_specs,
        out_specs=out_specs,
        out_shape=out_shape,
        scratch_shapes=scratch,
        input_output_aliases=aliases,
        compiler_params=_params(("arbitrary",)),
        name="in_proj",
    )(*args)


def _prompt_attn_kernel(q_ref, k_ref, vt_ref, c_ref, g_ref, o_ref, *, tq, hp):
    i = pl.program_id(2)
    lanes = [slice(e * HEAD_DIM, (e + 1) * HEAD_DIM) for e in range(hp)]
    reps = tq // LANES

    def update(e, start, carry, mask):
        m, acc = carry
        bias = jnp.tile(c_ref[e, pl.ds(start, tq), :], (1, reps))
        s = _dot_nt(k_ref[pl.ds(start, tq), lanes[e]], q_ref[:, lanes[e]]) - bias
        if mask is not None:
            s = jnp.where(mask, s, NEG_INF)
        m_new = jnp.maximum(m, jnp.max(s, axis=0, keepdims=True))
        p = jnp.exp2(s - m_new)
        acc = jnp.exp2(m - m_new) * acc + _dot(vt_ref[e, :, pl.ds(start, tq)], p.astype(bf16))
        return m_new, acc

    def body(j, carries):
        start = pl.multiple_of(j * tq, tq)
        return tuple(update(e, start, carries[e], None) for e in range(hp))

    init = (jnp.full((1, tq), NEG_INF, f32), jnp.zeros((VT_ROWS, tq), f32))
    carries = lax.fori_loop(0, i, body, (init,) * hp)
    key = lax.broadcasted_iota(jnp.int32, (tq, tq), 0)
    qry = lax.broadcasted_iota(jnp.int32, (tq, tq), 1)
    start = pl.multiple_of(i * tq, tq)
    for e in range(hp):
        _, acc = update(e, start, carries[e], key <= qry)
        ot = acc[:HEAD_DIM] / acc[HEAD_DIM:HEAD_DIM + 1]
        ot = ot * lax.rsqrt(jnp.mean(ot * ot, axis=0, keepdims=True) + EPS) * jnp.tile(g_ref[e], (1, reps))
        o_ref[:, lanes[e]] = ot.T.astype(bf16)


def _prompt_attn(layer, q, kb, vt, crep, g_rep, *, batch, seq, tq, hp):
    m = q.shape[0]
    nq = seq // tq
    width = hp * HEAD_DIM
    return pl.pallas_call(
        functools.partial(_prompt_attn_kernel, tq=tq, hp=hp),
        grid=(batch, N_HEADS // hp, nq),
        in_specs=[
            pl.BlockSpec((tq, width), lambda b, h, i: (b * nq + i, h)),
            pl.BlockSpec((seq, width), lambda b, h, i: (b, h)),
            pl.BlockSpec((hp, VT_ROWS, seq), lambda b, h, i: (h, 0, b)),
            pl.BlockSpec((hp, seq, LANES), lambda b, h, i: (h, b, 0)),
            pl.BlockSpec((None, hp, HEAD_DIM, LANES), lambda b, h, i: (layer, h, 0, 0)),
        ],
        out_specs=pl.BlockSpec((tq, width), lambda b, h, i: (b * nq + i, h)),
        out_shape=jax.ShapeDtypeStruct((m, ATT_WIDTH), bf16),
        compiler_params=_params(("arbitrary", "arbitrary", "arbitrary")),
        name="prompt_attn",
    )(q, kb, vt, crep, g_rep)


def _conv_kernel(u_ref, init_ref, w_ref, b_ref, g_ref, beta_ref, o_ref, ext_ref, sh_ref, *, tl, rc):
    j = pl.program_id(1)

    @pl.when(j == 0)
    def _():
        ext_ref[0:CONV_HALO, :] = init_ref[...]

    @pl.when(j > 0)
    def _():
        ext_ref[0:CONV_HALO, :] = ext_ref[tl:tl + CONV_HALO, :]

    ext_ref[CONV_HALO:CONV_HALO + tl, :] = u_ref[...]
    n_sh = tl + CONV_HALO - SUBLANES
    for s in range(1, SUBLANES):
        sh_ref[s - 1] = ext_ref[s:s + n_sh, :]
    first = CONV_HALO - (CONV_K - 1)
    for r in range(tl // rc):
        acc = jnp.zeros((rc // SUBLANES, SUBLANES, CONV_WIDTH), f32)
        for t in range(CONV_K):
            a, s = divmod(first + t, SUBLANES)
            lo = r * rc + a * SUBLANES
            x = ext_ref[lo:lo + rc, :] if s == 0 else sh_ref[s - 1, lo:lo + rc, :]
            acc = acc + w_ref[t][None] * x.reshape(rc // SUBLANES, SUBLANES, CONV_WIDTH)
        y = acc.reshape(rc, CONV_WIDTH) + b_ref[...]
        yc = y - jnp.mean(y, axis=-1, keepdims=True)
        yn = yc * lax.rsqrt(jnp.mean(yc * yc, axis=-1, keepdims=True) + EPS) * g_ref[...] + beta_ref[...]
        o_ref[r * rc:(r + 1) * rc, :] = (yn * jax.nn.sigmoid(yn)).astype(bf16)


def _conv(layer, u3, init, w, *, tl, rc):
    nb, seq, _ = u3.shape
    return pl.pallas_call(
        functools.partial(_conv_kernel, tl=tl, rc=rc),
        grid=(nb, seq // tl),
        in_specs=[
            pl.BlockSpec((None, tl, CONV_WIDTH), lambda b_, j: (b_, j, 0)),
            pl.BlockSpec((None, CONV_HALO, CONV_WIDTH), lambda b_, j: (b_, 0, 0)),
            _resident_layer(w["conv_w"], layer), _resident_layer(w["conv_b"], layer),
            _resident_layer(w["ln_g"], layer), _resident_layer(w["ln_b"], layer),
        ],
        out_specs=pl.BlockSpec((None, tl, CONV_WIDTH), lambda b_, j: (b_, j, 0)),
        out_shape=jax.ShapeDtypeStruct((nb, seq, CONV_WIDTH), bf16),
        scratch_shapes=[pltpu.VMEM((CONV_HALO + tl, CONV_WIDTH), f32),
                        pltpu.VMEM((SUBLANES - 1, tl + CONV_HALO - SUBLANES, CONV_WIDTH), f32)],
        compiler_params=_params(("arbitrary", "arbitrary")),
        name="conv",
    )(u3, init, w["conv_w"], w["conv_b"], w["ln_g"], w["ln_b"])


def _out_proj_kernel(x_ref, a_ref, cy_ref, w_ref, o_ref, *, nc):
    a = a_ref[...]
    cy = cy_ref[...]
    for c in range(D_MODEL // nc):
        sl = slice(c * nc, (c + 1) * nc)
        o_ref[:, sl] = (x_ref[:, sl] + _dot(a, w_ref[0:ATT_WIDTH, sl])
                        + _dot(cy, w_ref[ATT_WIDTH:D_MODEL, sl]))


def _out_proj(layer, x, attn, cy, w_out, *, tm):
    m = x.shape[0]
    return pl.pallas_call(
        functools.partial(_out_proj_kernel, nc=512),
        grid=(m // tm,),
        in_specs=[pl.BlockSpec((tm, D_MODEL), lambda i: (i, 0)),
                  pl.BlockSpec((tm, ATT_WIDTH), lambda i: (i, 0)),
                  pl.BlockSpec((tm, CONV_WIDTH), lambda i: (i, 0)),
                  _resident_layer(w_out, layer)],
        out_specs=pl.BlockSpec((tm, D_MODEL), lambda i: (i, 0)),
        out_shape=jax.ShapeDtypeStruct((m, D_MODEL), f32),
        compiler_params=_params(("arbitrary",)),
        name="out_proj",
    )(x, attn, cy, w_out)


def _mlp_kernel(x_ref, g_ref, wu_ref, wd_ref, o_ref, xn_ref, *, nc):
    f = pl.program_id(1)

    @pl.when(f == 0)
    def _():
        x = x_ref[...]
        xn_ref[...] = _rms(x, g_ref[...]).astype(bf16)
        o_ref[...] = x

    hid = jnp.square(jnp.maximum(_dot(xn_ref[...], wu_ref[...]), 0.0)).astype(bf16)
    for c in range(D_MODEL // nc):
        sl = slice(c * nc, (c + 1) * nc)
        o_ref[:, sl] += _dot(hid, wd_ref[:, sl])


def _mlp(layer, x, g, wu, wd, *, tm, tf):
    m = x.shape[0]
    return pl.pallas_call(
        functools.partial(_mlp_kernel, nc=512),
        grid=(m // tm, D_FF // tf),
        in_specs=[pl.BlockSpec((tm, D_MODEL), lambda i, f: (i, 0)),
                  _resident_layer(g, layer),
                  pl.BlockSpec((None, D_MODEL, tf), lambda i, f: (layer, 0, f)),
                  pl.BlockSpec((None, tf, D_MODEL), lambda i, f: (layer, f, 0))],
        out_specs=pl.BlockSpec((tm, D_MODEL), lambda i, f: (i, 0)),
        out_shape=jax.ShapeDtypeStruct((m, D_MODEL), f32),
        scratch_shapes=[pltpu.VMEM((tm, D_MODEL), bf16)],
        compiler_params=_params(("arbitrary", "arbitrary")),
        name="mlp",
    )(x, g, wu, wd)


def _ple_kernel(x_ref, pe_ref, g_ref, wg_ref, wp_ref, gf_ref, o_ref, *, nc, final):
    xn = _rms(x_ref[...], g_ref[...]).astype(bf16)
    pe = pe_ref[...].astype(bf16)
    for c in range(D_MODEL // nc):
        sl = slice(c * nc, (c + 1) * nc)
        gate = jax.nn.sigmoid(_dot(xn, wg_ref[:, sl]))
        o_ref[:, sl] = x_ref[:, sl] + _dot(pe, wp_ref[:, sl]) * gate
    if final:
        o_ref[...] = _rms(o_ref[...], gf_ref[...])


def _ple(layer, x, pe, g, wg, wp, gf, *, tm, final):
    m = x.shape[0]
    return pl.pallas_call(
        functools.partial(_ple_kernel, nc=512, final=final),
        grid=(m // tm,),
        in_specs=[pl.BlockSpec((tm, D_MODEL), lambda i: (i, 0)),
                  pl.BlockSpec((None, tm, pe.shape[2]), lambda i: (layer, i, 0)),
                  _resident_layer(g, layer), _resident_layer(wg, layer), _resident_layer(wp, layer),
                  _resident((1, D_MODEL))],
        out_specs=pl.BlockSpec((tm, D_MODEL), lambda i: (i, 0)),
        out_shape=jax.ShapeDtypeStruct((m, D_MODEL), f32),
        compiler_params=_params(("arbitrary",)),
        name="ple",
    )(x, pe, g, wg, wp, gf)


def _decode_kernel(pt_ref, q_ref, kn_ref, vn_ref, lfn_ref, gh_ref, *rest, pages_per_step, n_tok):
    del pt_ref
    g_ = pages_per_step
    k_refs, v_refs, lf_refs = rest[0:g_], rest[g_:2 * g_], rest[2 * g_:3 * g_]
    o_ref, m_ref, l_ref, acc_ref, carry_ref = rest[3 * g_:]
    j = pl.program_id(1)
    tp = q_ref.shape[1]
    page = lf_refs[0].shape[1]
    half = N_HEADS // 2
    order = [h for pair in range(half) for h in (pair, pair + half)]

    def online(s, pv_of_pair):
        m_old = m_ref[...]
        m_new = jnp.maximum(m_old, jnp.max(s, axis=-1, keepdims=True))
        p = jnp.exp(s - m_new)
        alpha = jnp.exp(m_old - m_new)
        l_ref[...] = alpha * l_ref[...] + jnp.sum(p, axis=-1, keepdims=True)
        pb = p.astype(bf16)
        pv = [pv_of_pair(pair, pb[2 * pair * tp:2 * (pair + 1) * tp, :]) for pair in range(half)]
        acc_ref[...] = alpha * acc_ref[...] + jnp.concatenate(pv, axis=0)
        m_ref[...] = m_new

    @pl.when(j == 0)
    def _():
        m_ref[...] = jnp.full_like(m_ref, NEG_INF)
        l_ref[...] = jnp.zeros_like(l_ref)
        acc_ref[...] = jnp.zeros_like(acc_ref)
        carry_ref[...] = jnp.zeros_like(carry_ref)
        src = lax.broadcasted_iota(jnp.int32, (LANES, LANES), 0)
        dst = lax.broadcasted_iota(jnp.int32, (LANES, LANES), 1)
        cn = _dot_exact(lfn_ref[...], (src <= dst).astype(f32))
        tok = lax.broadcasted_iota(jnp.int32, (tp, LANES), 0)
        key = lax.broadcasted_iota(jnp.int32, (tp, LANES), 1)
        causal = (key <= tok) & (key < n_tok)
        s_new = [jnp.where(causal, _dot_nt(q_ref[h], kn_ref[h]) - cn[h:h + 1, :], NEG_INF) for h in order]

        def pv_new(pair, pb):
            return jnp.concatenate([_dot(pb[0:tp, :], vn_ref[pair]), _dot(pb[tp:2 * tp, :], vn_ref[pair + half])],
                                   axis=0)

        online(jnp.concatenate(s_new, axis=0), pv_new)

    width = 2 * page
    src = lax.broadcasted_iota(jnp.int32, (page, 2 * width), 0)
    dst = lax.broadcasted_iota(jnp.int32, (page, 2 * width), 1)
    both = _dot_exact(jnp.concatenate([lf_refs[g][...] for g in range(g_)], axis=0),
                      ((src > (dst >> 1)) | (dst >= width)).astype(f32))
    run = carry_ref[...]
    r = [None] * g_
    for g in reversed(range(g_)):
        r[g] = both[g * N_HEADS:(g + 1) * N_HEADS, :width] + run
        run = run + both[g * N_HEADS:(g + 1) * N_HEADS, width:]
    carry_ref[...] = run

    def pair_rows(refs, pair):
        rows = [ref[pl.ds(pair, width, stride=half), :] for ref in refs]
        return jnp.concatenate(rows, axis=0).astype(bf16)

    row = lax.broadcasted_iota(jnp.int32, (2 * tp, g_ * width), 0)
    col = lax.broadcasted_iota(jnp.int32, (2 * tp, g_ * width), 1)
    own = (col & 1) == (row >= tp).astype(jnp.int32)
    s_pairs = []
    for pair in range(half):
        q2 = jnp.concatenate([q_ref[pair], q_ref[pair + half]], axis=0)
        bias = jnp.concatenate(
            [jnp.concatenate([jnp.broadcast_to(r[g][pair:pair + 1, :], (tp, width)),
                              jnp.broadcast_to(r[g][pair + half:pair + half + 1, :], (tp, width))], axis=0)
             for g in range(g_)], axis=1)
        s_pairs.append(jnp.where(own, _dot_nt(q2, pair_rows(k_refs, pair)) + bias, NEG_INF))
    online(jnp.concatenate(s_pairs, axis=0), lambda pair, pb: _dot(pb, pair_rows(v_refs, pair)))

    @pl.when(j == pl.num_programs(1) - 1)
    def _():
        o = acc_ref[...] / l_ref[...]
        for idx, h in enumerate(order):
            o_ref[h] = _rms(o[idx * tp:(idx + 1) * tp, :], gh_ref[h:h + 1, :]).astype(bf16)


def _decode_attn(layer, page_table, q, kn, vn, lfn, g_heads, cache_k, cache_v, cache_lf, *, pages_per_step,
                 n_tok):
    nb, n_pages = page_table.shape
    page = cache_lf.shape[3]
    g_ = pages_per_step
    n_groups = n_pages // g_
    tp = q.shape[2]

    def page_map(g, nd):
        def index_map(b, j, pt):
            return (layer, pt[b * n_pages + (n_groups - 1 - j) * g_ + g]) + (0,) * nd
        return index_map

    kv_spec = lambda g: pl.BlockSpec((None, None, page * N_HEADS, HEAD_DIM), page_map(g, 2))
    lf_spec = lambda g: pl.BlockSpec((None, None, N_HEADS, page), page_map(g, 2))
    per_b = lambda shape: pl.BlockSpec((None,) + shape, lambda b, j, pt: (b,) + (0,) * len(shape))
    grid_spec = pltpu.PrefetchScalarGridSpec(
        num_scalar_prefetch=1,
        grid=(nb, n_groups),
        in_specs=[per_b((N_HEADS, tp, HEAD_DIM)), per_b((N_HEADS, LANES, HEAD_DIM)),
                  per_b((N_HEADS, LANES, HEAD_DIM)), per_b((N_HEADS, LANES)),
                  pl.BlockSpec((None, N_HEADS, HEAD_DIM), lambda b, j, pt: (layer, 0, 0))]
                 + [kv_spec(g) for g in range(g_)] + [kv_spec(g) for g in range(g_)]
                 + [lf_spec(g) for g in range(g_)],
        out_specs=per_b((N_HEADS, tp, HEAD_DIM)),
        scratch_shapes=[pltpu.VMEM((N_HEADS * tp, 1), f32), pltpu.VMEM((N_HEADS * tp, 1), f32),
                        pltpu.VMEM((N_HEADS * tp, HEAD_DIM), f32), pltpu.VMEM((N_HEADS, 2 * page), f32)],
    )
    return pl.pallas_call(
        functools.partial(_decode_kernel, pages_per_step=g_, n_tok=n_tok),
        grid_spec=grid_spec,
        out_shape=jax.ShapeDtypeStruct((nb, N_HEADS, tp, HEAD_DIM), bf16),
        compiler_params=_params(("arbitrary", "arbitrary")),
        name="decode_attn",
    )(page_table.reshape(-1), q, kn, vn, lfn, g_heads,
      *([cache_k] * g_), *([cache_v] * g_), *([cache_lf] * g_))


def _stacked_weights(w_in, b_f, conv_w, conv_b, conv_ln_g, conv_ln_b, g_attn_out, w_out, g_mix, g_mlp,
                     w_up, w_down, g_ple, w_ple, w_ple_gate):
    a3 = 3 * ATT_WIDTH
    depth = w_in.shape[0]
    rows = lambda v: v.reshape(depth, 1, -1).astype(f32)
    return dict(
        wqkv=w_in[:, :, :a3].astype(bf16),
        wf=jnp.pad(jnp.tile(w_in[:, :, a3:a3 + N_HEADS], (1, 1, 3)),
                   ((0, 0), (0, 0), (0, LANES - 3 * N_HEADS))).astype(bf16),
        bf=jnp.pad(jnp.tile(b_f, (1, 3)), ((0, 0), (0, LANES - 3 * N_HEADS))).reshape(depth, 1, LANES),
        wcv=w_in[:, :, a3 + N_HEADS:a3 + N_HEADS + CONV_WIDTH].astype(bf16),
        wcg=w_in[:, :, a3 + N_HEADS + CONV_WIDTH:].astype(bf16),
        conv_w=jnp.broadcast_to(conv_w[:, :, None, :], (depth, CONV_K, SUBLANES, CONV_WIDTH)),
        conv_b=rows(conv_b), ln_g=rows(conv_ln_g), ln_b=rows(conv_ln_b),
        g_heads=g_attn_out.reshape(depth, N_HEADS, HEAD_DIM),
        g_rep=jnp.broadcast_to(g_attn_out.reshape(depth, N_HEADS, HEAD_DIM, 1),
                               (depth, N_HEADS, HEAD_DIM, LANES)),
        w_out=w_out.astype(bf16), g_mix=rows(g_mix), g_mlp=rows(g_mlp),
        w_up=w_up.astype(bf16), w_down=w_down.astype(bf16),
        g_ple=rows(g_ple), w_ple=w_ple.astype(bf16), w_gate=w_ple_gate.astype(bf16),
    )


def _finish(layer, x, attn, cy, pe, w, gf, *, tm, tf, final):
    x = _out_proj(layer, x, attn, cy, w["w_out"], tm=tm)
    x = _mlp(layer, x, w["g_mlp"], w["w_up"], w["w_down"], tm=tm, tf=tf)
    return _ple(layer, x, pe, w["g_ple"], w["w_gate"], w["w_ple"], gf, tm=tm, final=final)


def kernel(x_prompt, x_sample, cache_k, cache_v, cache_logf, state_conv, page_table, p_prompt, p_sample,
           w_in, b_f, conv_w, conv_b, conv_ln_g, conv_ln_b, g_attn_out, w_out, g_mix, g_mlp, w_up, w_down,
           g_ple, w_ple, w_ple_gate, g_final):
    depth = w_in.shape[0]
    nb, seq, _ = x_prompt.shape
    db, dseq, _ = x_sample.shape
    mp, ms = nb * seq, db * dseq
    keep = CONV_K - 1
    xp = x_prompt.reshape(mp, D_MODEL)
    xs = x_sample.reshape(ms, D_MODEL)
    gf = g_final.reshape(1, D_MODEL)
    w = _stacked_weights(w_in, b_f, conv_w, conv_b, conv_ln_g, conv_ln_b, g_attn_out, w_out, g_mix, g_mlp,
                         w_up, w_down, g_ple, w_ple, w_ple_gate)
    pe_prompt = p_prompt.reshape(depth, mp, -1)
    pe_sample = p_sample.reshape(depth, ms, -1)
    cache_lf = jnp.swapaxes(cache_logf, 2, 3)
    cache_k = cache_k.reshape(cache_k.shape[0], cache_k.shape[1], -1, HEAD_DIM)
    cache_v = cache_v.reshape(cache_v.shape[0], cache_v.shape[1], -1, HEAD_DIM)
    zero_state = jnp.zeros((nb, CONV_HALO, CONV_WIDTH), f32)
    heads_first = lambda a: a.reshape(db, dseq, N_HEADS, -1).transpose(0, 2, 1, 3)
    pad_tok = lambda a, n: jnp.pad(a, ((0, 0), (0, 0), (0, n - dseq), (0, 0)))
    kv_prompt = None
    outs = {name: [] for name in ("lfp", "cvp", "ks", "vs", "lfs", "cvs")}
    for l in range(depth):
        final = l == depth - 1

        q, k_all, v_all, kb, vt, lf, crep, u = _in_proj(l, xp, w["g_mix"], w, kv_prompt, tm=512, seg_len=seq,
                                                        prompt=True)
        kv_prompt = (k_all, v_all)
        attn = _prompt_attn(l, q, kb, vt, crep, w["g_rep"], batch=nb, seq=seq, tq=1024, hp=2)
        u3 = u.reshape(nb, seq, CONV_WIDTH)
        cy = _conv(l, u3, zero_state, w, tl=512, rc=32).reshape(mp, CONV_WIDTH)
        xp = _finish(l, xp, attn, cy, pe_prompt, w, gf, tm=512, tf=1024, final=final)
        outs["lfp"].append(lf.reshape(N_HEADS, nb, seq).transpose(1, 2, 0))
        outs["cvp"].append(u3[:, seq - keep:, :])

        q, k, v, lf, u = _in_proj(l, xs, w["g_mix"], w, None, tm=ms, seg_len=dseq, prompt=False)
        lf = lf[:, :N_HEADS].reshape(db, dseq, N_HEADS)
        attn = _decode_attn(
            l, page_table, pad_tok(heads_first(q), BF16_ROWS),
            pad_tok(heads_first(k), LANES).astype(bf16), pad_tok(heads_first(v), LANES).astype(bf16),
            jnp.pad(lf.transpose(0, 2, 1), ((0, 0), (0, 0), (0, LANES - dseq))),
            w["g_heads"], cache_k, cache_v, cache_lf, pages_per_step=8, n_tok=dseq)
        attn = attn[:, :, :dseq, :].transpose(0, 2, 1, 3).reshape(ms, ATT_WIDTH)
        u3 = u.reshape(db, dseq, CONV_WIDTH)
        state = jnp.pad(state_conv[l], ((0, 0), (CONV_HALO - keep, 0), (0, 0)))
        cy = _conv(l, u3, state, w, tl=dseq, rc=dseq).reshape(ms, CONV_WIDTH)
        xs = _finish(l, xs, attn, cy, pe_sample, w, gf, tm=ms, tf=1024, final=final)
        outs["ks"].append(k.reshape(db, dseq, N_HEADS, HEAD_DIM))
        outs["vs"].append(v.reshape(db, dseq, N_HEADS, HEAD_DIM))
        outs["lfs"].append(lf)
        outs["cvs"].append(jnp.concatenate([state_conv[l], u3], axis=1)[:, -keep:, :])
    st = lambda name: jnp.stack(outs[name])
    k_all, v_all = kv_prompt
    return (xp.reshape(nb, seq, D_MODEL), xs.reshape(db, dseq, D_MODEL),
            k_all.reshape(depth, nb, seq, N_HEADS, HEAD_DIM), v_all.reshape(depth, nb, seq, N_HEADS, HEAD_DIM),
            st("lfp"), st("cvp"), st("ks"), st("vs"), st("lfs"), st("cvs"))
```

```python
import functools

import jax
import jax.numpy as jnp
from jax import lax
from jax.experimental import pallas as pl
from jax.experimental.pallas import tpu as pltpu

D_MODEL = 2048
N_HEADS = 8
HEAD_DIM = 128
ATT_WIDTH = N_HEADS * HEAD_DIM
CONV_WIDTH = D_MODEL - ATT_WIDTH
CONV_K = 31
D_FF = 4 * D_MODEL
EPS = 1e-6

LANES = 128
SUBLANES = 8
BF16_ROWS = 16
CONV_HALO = 32
VT_ROWS = HEAD_DIM + BF16_ROWS
LOG2E = 1.4426950408889634
V7X_VMEM_BYTES = 64 * 1024 * 1024
VMEM_LIMIT = V7X_VMEM_BYTES - 8 * 1024 * 1024
NEG_INF = float("-inf")
NT_DIMS = (((1,), (1,)), ((), ()))

f32 = jnp.float32
bf16 = jnp.bfloat16


def _dot(a, b):
    return jnp.dot(a, b, preferred_element_type=f32)


def _dot_nt(a, b):
    return lax.dot_general(a, b, NT_DIMS, preferred_element_type=f32)


def _dot_exact(a, b):
    return jnp.dot(a, b, preferred_element_type=f32, precision=lax.Precision.HIGHEST)


def _rms(x, g):
    return x * lax.rsqrt(jnp.mean(x * x, axis=-1, keepdims=True) + EPS) * g


def _log_sigmoid(x):
    return jnp.minimum(x, 0.0) - jnp.log1p(jnp.exp(-jnp.abs(x)))


def _resident(shape):
    nd = len(shape)
    return pl.BlockSpec(shape, lambda *_: (0,) * nd, pipeline_mode=pl.Buffered(1))


def _resident_layer(arr, layer):
    nd = arr.ndim - 1
    return pl.BlockSpec((None,) + arr.shape[1:], lambda *_: (layer,) + (0,) * nd,
                        pipeline_mode=pl.Buffered(1))


def _params(sem):
    return pltpu.CompilerParams(dimension_semantics=sem, vmem_limit_bytes=VMEM_LIMIT)


def _in_proj_kernel(x_ref, g_ref, wqkv_ref, wf_ref, bf_ref, wcv_ref, wcg_ref, *rest, tm, seg_len, nc, prompt):
    if prompt:
        q_ref, k_ref, v_ref, kb_ref, vt_ref, lf_ref, crep_ref, u_ref, carry_ref = rest[-9:]
    else:
        q_ref, k_ref, v_ref, lf_ref, u_ref = rest
    if prompt:
        @pl.when(lax.rem(pl.program_id(0), seg_len // tm) == 0)
        def _():
            carry_ref[...] = jnp.zeros_like(carry_ref)

    xn = _rms(x_ref[...], g_ref[...]).astype(bf16)

    lf = _log_sigmoid(_dot(xn, wf_ref[...]) + bf_ref[...])
    if prompt:
        lane = lax.broadcasted_iota(jnp.int32, (tm, LANES), 1)
        hi = lf.astype(bf16).astype(f32)
        mid = (lf - hi).astype(bf16).astype(f32)
        lo = lf - hi - mid
        parts = jnp.where(lane < N_HEADS, hi,
                          jnp.where(lane < 2 * N_HEADS, mid, jnp.where(lane < 3 * N_HEADS, lo, 0.0)))
        row = lax.broadcasted_iota(jnp.int32, (tm, tm), 0)
        col = lax.broadcasted_iota(jnp.int32, (tm, tm), 1)
        cp = _dot((row >= col).astype(bf16), parts.astype(bf16)) + carry_ref[...]
        carry_ref[...] = cp[tm - 1:tm, :]
        lf_ref[...] = lf.T[:N_HEADS, :]
        for h in range(N_HEADS):
            c_h = cp[:, h:h + 1] + cp[:, h + N_HEADS:h + N_HEADS + 1] + cp[:, h + 2 * N_HEADS:h + 2 * N_HEADS + 1]
            crep_ref[h] = jnp.broadcast_to(c_h * LOG2E, (tm, LANES))
    else:
        lf_ref[...] = lf

    scale = HEAD_DIM ** -0.5 * (LOG2E if prompt else 1.0)
    for c in range(ATT_WIDTH // nc):
        sl = slice(c * nc, (c + 1) * nc)
        q_ref[:, sl] = (_dot(xn, wqkv_ref[:, sl]) * scale).astype(bf16)
        kk = _dot(xn, wqkv_ref[:, ATT_WIDTH + c * nc:ATT_WIDTH + (c + 1) * nc])
        k_ref[:, sl] = kk
        vv = _dot(xn, wqkv_ref[:, 2 * ATT_WIDTH + c * nc:2 * ATT_WIDTH + (c + 1) * nc])
        v_ref[:, sl] = vv
        if prompt:
            kb_ref[:, sl] = kk.astype(bf16)
            for hh in range(nc // HEAD_DIM):
                head = c * (nc // HEAD_DIM) + hh
                vt_ref[head, 0:HEAD_DIM, :] = vv[:, hh * HEAD_DIM:(hh + 1) * HEAD_DIM].T.astype(bf16)
                vt_ref[head, HEAD_DIM:, :] = jnp.ones((BF16_ROWS, tm), bf16)
    for c in range(CONV_WIDTH // nc):
        sl = slice(c * nc, (c + 1) * nc)
        u_ref[:, sl] = _dot(xn, wcv_ref[:, sl]) * jax.nn.sigmoid(_dot(xn, wcg_ref[:, sl]))


def _in_proj(layer, x, g, w, kv_prev, *, tm, seg_len, prompt):
    m = x.shape[0]
    depth = w["wqkv"].shape[0]
    nc = 512
    rows = lambda n: pl.BlockSpec((tm, n), lambda i: (i, 0))
    heads_rows = pl.BlockSpec((N_HEADS, tm), lambda i: (0, i))
    in_specs = [rows(D_MODEL), _resident_layer(g, layer), _resident_layer(w["wqkv"], layer),
                _resident_layer(w["wf"], layer), _resident_layer(w["bf"], layer),
                _resident_layer(w["wcv"], layer), _resident_layer(w["wcg"], layer)]
    args = [x, g, w["wqkv"], w["wf"], w["bf"], w["wcv"], w["wcg"]]
    aliases = {}
    if prompt:
        stacked = pl.BlockSpec((None, tm, ATT_WIDTH), lambda i: (layer, i, 0))
        out_shape = (
            jax.ShapeDtypeStruct((m, ATT_WIDTH), bf16),
            jax.ShapeDtypeStruct((depth, m, ATT_WIDTH), f32),
            jax.ShapeDtypeStruct((depth, m, ATT_WIDTH), f32),
            jax.ShapeDtypeStruct((m, ATT_WIDTH), bf16),
            jax.ShapeDtypeStruct((N_HEADS, VT_ROWS, m), bf16),
            jax.ShapeDtypeStruct((N_HEADS, m), f32),
            jax.ShapeDtypeStruct((N_HEADS, m, LANES), f32),
            jax.ShapeDtypeStruct((m, CONV_WIDTH), f32),
        )
        out_specs = (rows(ATT_WIDTH), stacked, stacked, rows(ATT_WIDTH),
                     pl.BlockSpec((N_HEADS, VT_ROWS, tm), lambda i: (0, 0, i)), heads_rows,
                     pl.BlockSpec((N_HEADS, tm, LANES), lambda i: (0, i, 0)), rows(CONV_WIDTH))
        scratch = [pltpu.VMEM((1, LANES), f32)]
        if kv_prev is not None:
            in_specs += [pl.BlockSpec(memory_space=pl.ANY)] * 2
            args += list(kv_prev)
            aliases = {len(args) - 2: 1, len(args) - 1: 2}
    else:
        out_shape = (
            jax.ShapeDtypeStruct((m, ATT_WIDTH), bf16),
            jax.ShapeDtypeStruct((m, ATT_WIDTH), f32),
            jax.ShapeDtypeStruct((m, ATT_WIDTH), f32),
            jax.ShapeDtypeStruct((m, LANES), f32),
            jax.ShapeDtypeStruct((m, CONV_WIDTH), f32),
        )
        out_specs = (rows(ATT_WIDTH), rows(ATT_WIDTH), rows(ATT_WIDTH), rows(LANES), rows(CONV_WIDTH))
        scratch = []
    return pl.pallas_call(
        functools.partial(_in_proj_kernel, tm=tm, seg_len=seg_len, nc=nc, prompt=prompt),
        grid=(m // tm,),
        in_specs=in_specs,
        out_specs=out_specs,
        out_shape=out_shape,
        scratch_shapes=scratch,
        input_output_aliases=aliases,
        compiler_params=_params(("arbitrary",)),
        name="in_proj",
    )(*args)


def _prompt_attn_kernel(q_ref, k_ref, vt_ref, c_ref, g_ref, o_ref, *, tq, hp):
    i = pl.program_id(2)
    lanes = [slice(e * HEAD_DIM, (e + 1) * HEAD_DIM) for e in range(hp)]
    reps = tq // LANES

    def update(e, start, carry, mask):
        m, acc = carry
        bias = jnp.tile(c_ref[e, pl.ds(start, tq), :], (1, reps))
        s = _dot_nt(k_ref[pl.ds(start, tq), lanes[e]], q_ref[:, lanes[e]]) - bias
        if mask is not None:
            s = jnp.where(mask, s, NEG_INF)
        m_new = jnp.maximum(m, jnp.max(s, axis=0, keepdims=True))
        p = jnp.exp2(s - m_new)
        acc = jnp.exp2(m - m_new) * acc + _dot(vt_ref[e, :, pl.ds(start, tq)], p.astype(bf16))
        return m_new, acc

    def body(j, carries):
        start = pl.multiple_of(j * tq, tq)
        return tuple(update(e, start, carries[e], None) for e in range(hp))

    init = (jnp.full((1, tq), NEG_INF, f32), jnp.zeros((VT_ROWS, tq), f32))
    carries = lax.fori_loop(0, i, body, (init,) * hp)
    key = lax.broadcasted_iota(jnp.int32, (tq, tq), 0)
    qry = lax.broadcasted_iota(jnp.int32, (tq, tq), 1)
    start = pl.multiple_of(i * tq, tq)
    for e in range(hp):
        _, acc = update(e, start, carries[e], key <= qry)
        ot = acc[:HEAD_DIM] / acc[HEAD_DIM:HEAD_DIM + 1]
        ot = ot * lax.rsqrt(jnp.mean(ot * ot, axis=0, keepdims=True) + EPS) * jnp.tile(g_ref[e], (1, reps))
        o_ref[:, lanes[e]] = ot.T.astype(bf16)


def _prompt_attn(layer, q, kb, vt, crep, g_rep, *, batch, seq, tq, hp):
    m = q.shape[0]
    nq = seq // tq
    width = hp * HEAD_DIM
    return pl.pallas_call(
        functools.partial(_prompt_attn_kernel, tq=tq, hp=hp),
        grid=(batch, N_HEADS // hp, nq),
        in_specs=[
            pl.BlockSpec((tq, width), lambda b, h, i: (b * nq + i, h)),
            pl.BlockSpec((seq, width), lambda b, h, i: (b, h)),
            pl.BlockSpec((hp, VT_ROWS, seq), lambda b, h, i: (h, 0, b)),
            pl.BlockSpec((hp, seq, LANES), lambda b, h, i: (h, b, 0)),
            pl.BlockSpec((None, hp, HEAD_DIM, LANES), lambda b, h, i: (layer, h, 0, 0)),
        ],
        out_specs=pl.BlockSpec((tq, width), lambda b, h, i: (b * nq + i, h)),
        out_shape=jax.ShapeDtypeStruct((m, ATT_WIDTH), bf16),
        compiler_params=_params(("arbitrary", "arbitrary", "arbitrary")),
        name="prompt_attn",
    )(q, kb, vt, crep, g_rep)


def _conv_kernel(u_ref, init_ref, w_ref, b_ref, g_ref, beta_ref, o_ref, ext_ref, sh_ref, *, tl, rc):
    j = pl.program_id(1)

    @pl.when(j == 0)
    def _():
        ext_ref[0:CONV_HALO, :] = init_ref[...]

    @pl.when(j > 0)
    def _():
        ext_ref[0:CONV_HALO, :] = ext_ref[tl:tl + CONV_HALO, :]

    ext_ref[CONV_HALO:CONV_HALO + tl, :] = u_ref[...]
    n_sh = tl + CONV_HALO - SUBLANES
    for s in range(1, SUBLANES):
        sh_ref[s - 1] = ext_ref[s:s + n_sh, :]
    first = CONV_HALO - (CONV_K - 1)
    for r in range(tl // rc):
        acc = jnp.zeros((rc // SUBLANES, SUBLANES, CONV_WIDTH), f32)
        for t in range(CONV_K):
            a, s = divmod(first + t, SUBLANES)
            lo = r * rc + a * SUBLANES
            x = ext_ref[lo:lo + rc, :] if s == 0 else sh_ref[s - 1, lo:lo + rc, :]
            acc = acc + w_ref[t][None] * x.reshape(rc // SUBLANES, SUBLANES, CONV_WIDTH)
        y = acc.reshape(rc, CONV_WIDTH) + b_ref[...]
        yc = y - jnp.mean(y, axis=-1, keepdims=True)
        yn = yc * lax.rsqrt(jnp.mean(yc * yc, axis=-1, keepdims=True) + EPS) * g_ref[...] + beta_ref[...]
        o_ref[r * rc:(r + 1) * rc, :] = (yn * jax.nn.sigmoid(yn)).astype(bf16)


def _conv(layer, u3, init, w, *, tl, rc):
    nb, seq, _ = u3.shape
    return pl.pallas_call(
        functools.partial(_conv_kernel, tl=tl, rc=rc),
        grid=(nb, seq // tl),
        in_specs=[
            pl.BlockSpec((None, tl, CONV_WIDTH), lambda b_, j: (b_, j, 0)),
            pl.BlockSpec((None, CONV_HALO, CONV_WIDTH), lambda b_, j: (b_, 0, 0)),
            _resident_layer(w["conv_w"], layer), _resident_layer(w["conv_b"], layer),
            _resident_layer(w["ln_g"], layer), _resident_layer(w["ln_b"], layer),
        ],
        out_specs=pl.BlockSpec((None, tl, CONV_WIDTH), lambda b_, j: (b_, j, 0)),
        out_shape=jax.ShapeDtypeStruct((nb, seq, CONV_WIDTH), bf16),
        scratch_shapes=[pltpu.VMEM((CONV_HALO + tl, CONV_WIDTH), f32),
                        pltpu.VMEM((SUBLANES - 1, tl + CONV_HALO - SUBLANES, CONV_WIDTH), f32)],
        compiler_params=_params(("arbitrary", "arbitrary")),
        name="conv",
    )(u3, init, w["conv_w"], w["conv_b"], w["ln_g"], w["ln_b"])


def _out_proj_kernel(x_ref, a_ref, cy_ref, w_ref, o_ref, *, nc):
    a = a_ref[...]
    cy = cy_ref[...]
    for c in range(D_MODEL // nc):
        sl = slice(c * nc, (c + 1) * nc)
        o_ref[:, sl] = (x_ref[:, sl] + _dot(a, w_ref[0:ATT_WIDTH, sl])
                        + _dot(cy, w_ref[ATT_WIDTH:D_MODEL, sl]))


def _out_proj(layer, x, attn, cy, w_out, *, tm):
    m = x.shape[0]
    return pl.pallas_call(
        functools.partial(_out_proj_kernel, nc=512),
        grid=(m // tm,),
        in_specs=[pl.BlockSpec((tm, D_MODEL), lambda i: (i, 0)),
                  pl.BlockSpec((tm, ATT_WIDTH), lambda i: (i, 0)),
                  pl.BlockSpec((tm, CONV_WIDTH), lambda i: (i, 0)),
                  _resident_layer(w_out, layer)],
        out_specs=pl.BlockSpec((tm, D_MODEL), lambda i: (i, 0)),
        out_shape=jax.ShapeDtypeStruct((m, D_MODEL), f32),
        compiler_params=_params(("arbitrary",)),
        name="out_proj",
    )(x, attn, cy, w_out)


def _mlp_kernel(x_ref, g_ref, wu_ref, wd_ref, o_ref, xn_ref, *, nc):
    f = pl.program_id(1)

    @pl.when(f == 0)
    def _():
        x = x_ref[...]
        xn_ref[...] = _rms(x, g_ref[...]).astype(bf16)
        o_ref[...] = x

    hid = jnp.square(jnp.maximum(_dot(xn_ref[...], wu_ref[...]), 0.0)).astype(bf16)
    for c in range(D_MODEL // nc):
        sl = slice(c * nc, (c + 1) * nc)
        o_ref[:, sl] += _dot(hid, wd_ref[:, sl])


def _mlp(layer, x, g, wu, wd, *, tm, tf):
    m = x.shape[0]
    return pl.pallas_call(
        functools.partial(_mlp_kernel, nc=512),
        grid=(m // tm, D_FF // tf),
        in_specs=[pl.BlockSpec((tm, D_MODEL), lambda i, f: (i, 0)),
                  _resident_layer(g, layer),
                  pl.BlockSpec((None, D_MODEL, tf), lambda i, f: (layer, 0, f)),
                  pl.BlockSpec((None, tf, D_MODEL), lambda i, f: (layer, f, 0))],
        out_specs=pl.BlockSpec((tm, D_MODEL), lambda i, f: (i, 0)),
        out_shape=jax.ShapeDtypeStruct((m, D_MODEL), f32),
        scratch_shapes=[pltpu.VMEM((tm, D_MODEL), bf16)],
        compiler_params=_params(("arbitrary", "arbitrary")),
        name="mlp",
    )(x, g, wu, wd)


def _ple_kernel(x_ref, pe_ref, g_ref, wg_ref, wp_ref, gf_ref, o_ref, *, nc, final):
    xn = _rms(x_ref[...], g_ref[...]).astype(bf16)
    pe = pe_ref[...].astype(bf16)
    for c in range(D_MODEL // nc):
        sl = slice(c * nc, (c + 1) * nc)
        gate = jax.nn.sigmoid(_dot(xn, wg_ref[:, sl]))
        o_ref[:, sl] = x_ref[:, sl] + _dot(pe, wp_ref[:, sl]) * gate
    if final:
        o_ref[...] = _rms(o_ref[...], gf_ref[...])


def _ple(layer, x, pe, g, wg, wp, gf, *, tm, final):
    m = x.shape[0]
    return pl.pallas_call(
        functools.partial(_ple_kernel, nc=512, final=final),
        grid=(m // tm,),
        in_specs=[pl.BlockSpec((tm, D_MODEL), lambda i: (i, 0)),
                  pl.BlockSpec((None, tm, pe.shape[2]), lambda i: (layer, i, 0)),
                  _resident_layer(g, layer), _resident_layer(wg, layer), _resident_layer(wp, layer),
                  _resident((1, D_MODEL))],
        out_specs=pl.BlockSpec((tm, D_MODEL), lambda i: (i, 0)),
        out_shape=jax.ShapeDtypeStruct((m, D_MODEL), f32),
        compiler_params=_params(("arbitrary",)),
        name="ple",
    )(x, pe, g, wg, wp, gf)


def _decode_kernel(pt_ref, q_ref, kn_ref, vn_ref, lfn_ref, gh_ref, *rest, pages_per_step, n_tok):
    del pt_ref
    g_ = pages_per_step
    k_refs, v_refs, lf_refs = rest[0:g_], rest[g_:2 * g_], rest[2 * g_:3 * g_]
    o_ref, m_ref, l_ref, acc_ref, carry_ref = rest[3 * g_:]
    j = pl.program_id(1)
    tp = q_ref.shape[1]
    page = lf_refs[0].shape[1]
    half = N_HEADS // 2
    order = [h for pair in range(half) for h in (pair, pair + half)]

    def online(s, pv_of_pair):
        m_old = m_ref[...]
        m_new = jnp.maximum(m_old, jnp.max(s, axis=-1, keepdims=True))
        p = jnp.exp(s - m_new)
        alpha = jnp.exp(m_old - m_new)
        l_ref[...] = alpha * l_ref[...] + jnp.sum(p, axis=-1, keepdims=True)
        pb = p.astype(bf16)
        pv = [pv_of_pair(pair, pb[2 * pair * tp:2 * (pair + 1) * tp, :]) for pair in range(half)]
        acc_ref[...] = alpha * acc_ref[...] + jnp.concatenate(pv, axis=0)
        m_ref[...] = m_new

    @pl.when(j == 0)
    def _():
        m_ref[...] = jnp.full_like(m_ref, NEG_INF)
        l_ref[...] = jnp.zeros_like(l_ref)
        acc_ref[...] = jnp.zeros_like(acc_ref)
        carry_ref[...] = jnp.zeros_like(carry_ref)
        src = lax.broadcasted_iota(jnp.int32, (LANES, LANES), 0)
        dst = lax.broadcasted_iota(jnp.int32, (LANES, LANES), 1)
        cn = _dot_exact(lfn_ref[...], (src <= dst).astype(f32))
        tok = lax.broadcasted_iota(jnp.int32, (tp, LANES), 0)
        key = lax.broadcasted_iota(jnp.int32, (tp, LANES), 1)
        causal = (key <= tok) & (key < n_tok)
        s_new = [jnp.where(causal, _dot_nt(q_ref[h], kn_ref[h]) - cn[h:h + 1, :], NEG_INF) for h in order]

        def pv_new(pair, pb):
            return jnp.concatenate([_dot(pb[0:tp, :], vn_ref[pair]), _dot(pb[tp:2 * tp, :], vn_ref[pair + half])],
                                   axis=0)

        online(jnp.concatenate(s_new, axis=0), pv_new)

    width = 2 * page
    src = lax.broadcasted_iota(jnp.int32, (page, 2 * width), 0)
    dst = lax.broadcasted_iota(jnp.int32, (page, 2 * width), 1)
    both = _dot_exact(jnp.concatenate([lf_refs[g][...] for g in range(g_)], axis=0),
                      ((src > (dst >> 1)) | (dst >= width)).astype(f32))
    run = carry_ref[...]
    r = [None] * g_
    for g in reversed(range(g_)):
        r[g] = both[g * N_HEADS:(g + 1) * N_HEADS, :width] + run
        run = run + both[g * N_HEADS:(g + 1) * N_HEADS, width:]
    carry_ref[...] = run

    def pair_rows(refs, pair):
        rows = [ref[pl.ds(pair, width, stride=half), :] for ref in refs]
        return jnp.concatenate(rows, axis=0).astype(bf16)

    row = lax.broadcasted_iota(jnp.int32, (2 * tp, g_ * width), 0)
    col = lax.broadcasted_iota(jnp.int32, (2 * tp, g_ * width), 1)
    own = (col & 1) == (row >= tp).astype(jnp.int32)
    s_pairs = []
    for pair in range(half):
        q2 = jnp.concatenate([q_ref[pair], q_ref[pair + half]], axis=0)
        bias = jnp.concatenate(
            [jnp.concatenate([jnp.broadcast_to(r[g][pair:pair + 1, :], (tp, width)),
                              jnp.broadcast_to(r[g][pair + half:pair + half + 1, :], (tp, width))], axis=0)
             for g in range(g_)], axis=1)
        s_pairs.append(jnp.where(own, _dot_nt(q2, pair_rows(k_refs, pair)) + bias, NEG_INF))
    online(jnp.concatenate(s_pairs, axis=0), lambda pair, pb: _dot(pb, pair_rows(v_refs, pair)))

    @pl.when(j == pl.num_programs(1) - 1)
    def _():
        o = acc_ref[...] / l_ref[...]
        for idx, h in enumerate(order):
            o_ref[h] = _rms(o[idx * tp:(idx + 1) * tp, :], gh_ref[h:h + 1, :]).astype(bf16)


def _decode_attn(layer, page_table, q, kn, vn, lfn, g_heads, cache_k, cache_v, cache_lf, *, pages_per_step,
                 n_tok):
    nb, n_pages = page_table.shape
    page = cache_lf.shape[3]
    g_ = pages_per_step
    n_groups = n_pages // g_
    tp = q.shape[2]

    def page_map(g, nd):
        def index_map(b, j, pt):
            return (layer, pt[b * n_pages + (n_groups - 1 - j) * g_ + g]) + (0,) * nd
        return index_map

    kv_spec = lambda g: pl.BlockSpec((None, None, page * N_HEADS, HEAD_DIM), page_map(g, 2))
    lf_spec = lambda g: pl.BlockSpec((None, None, N_HEADS, page), page_map(g, 2))
    per_b = lambda shape: pl.BlockSpec((None,) + shape, lambda b, j, pt: (b,) + (0,) * len(shape))
    grid_spec = pltpu.PrefetchScalarGridSpec(
        num_scalar_prefetch=1,
        grid=(nb, n_groups),
        in_specs=[per_b((N_HEADS, tp, HEAD_DIM)), per_b((N_HEADS, LANES, HEAD_DIM)),
                  per_b((N_HEADS, LANES, HEAD_DIM)), per_b((N_HEADS, LANES)),
                  pl.BlockSpec((None, N_HEADS, HEAD_DIM), lambda b, j, pt: (layer, 0, 0))]
                 + [kv_spec(g) for g in range(g_)] + [kv_spec(g) for g in range(g_)]
                 + [lf_spec(g) for g in range(g_)],
        out_specs=per_b((N_HEADS, tp, HEAD_DIM)),
        scratch_shapes=[pltpu.VMEM((N_HEADS * tp, 1), f32), pltpu.VMEM((N_HEADS * tp, 1), f32),
                        pltpu.VMEM((N_HEADS * tp, HEAD_DIM), f32), pltpu.VMEM((N_HEADS, 2 * page), f32)],
    )
    return pl.pallas_call(
        functools.partial(_decode_kernel, pages_per_step=g_, n_tok=n_tok),
        grid_spec=grid_spec,
        out_shape=jax.ShapeDtypeStruct((nb, N_HEADS, tp, HEAD_DIM), bf16),
        compiler_params=_params(("arbitrary", "arbitrary")),
        name="decode_attn",
    )(page_table.reshape(-1), q, kn, vn, lfn, g_heads,
      *([cache_k] * g_), *([cache_v] * g_), *([cache_lf] * g_))


def _stacked_weights(w_in, b_f, conv_w, conv_b, conv_ln_g, conv_ln_b, g_attn_out, w_out, g_mix, g_mlp,
                     w_up, w_down, g_ple, w_ple, w_ple_gate):
    a3 = 3 * ATT_WIDTH
    depth = w_in.shape[0]
    rows = lambda v: v.reshape(depth, 1, -1).astype(f32)
    return dict(
        wqkv=w_in[:, :, :a3].astype(bf16),
        wf=jnp.pad(jnp.tile(w_in[:, :, a3:a3 + N_HEADS], (1, 1, 3)),
                   ((0, 0), (0, 0), (0, LANES - 3 * N_HEADS))).astype(bf16),
        bf=jnp.pad(jnp.tile(b_f, (1, 3)), ((0, 0), (0, LANES - 3 * N_HEADS))).reshape(depth, 1, LANES),
        wcv=w_in[:, :, a3 + N_HEADS:a3 + N_HEADS + CONV_WIDTH].astype(bf16),
        wcg=w_in[:, :, a3 + N_HEADS + CONV_WIDTH:].astype(bf16),
        conv_w=jnp.broadcast_to(conv_w[:, :, None, :], (depth, CONV_K, SUBLANES, CONV_WIDTH)),
        conv_b=rows(conv_b), ln_g=rows(conv_ln_g), ln_b=rows(conv_ln_b),
        g_heads=g_attn_out.reshape(depth, N_HEADS, HEAD_DIM),
        g_rep=jnp.broadcast_to(g_attn_out.reshape(depth, N_HEADS, HEAD_DIM, 1),
                               (depth, N_HEADS, HEAD_DIM, LANES)),
        w_out=w_out.astype(bf16), g_mix=rows(g_mix), g_mlp=rows(g_mlp),
        w_up=w_up.astype(bf16), w_down=w_down.astype(bf16),
        g_ple=rows(g_ple), w_ple=w_ple.astype(bf16), w_gate=w_ple_gate.astype(bf16),
    )


def _finish(layer, x, attn, cy, pe, w, gf, *, tm, tf, final):
    x = _out_proj(layer, x, attn, cy, w["w_out"], tm=tm)
    x = _mlp(layer, x, w["g_mlp"], w["w_up"], w["w_down"], tm=tm, tf=tf)
    return _ple(layer, x, pe, w["g_ple"], w["w_gate"], w["w_ple"], gf, tm=tm, final=final)


def kernel(x_prompt, x_sample, cache_k, cache_v, cache_logf, state_conv, page_table, p_prompt, p_sample,
           w_in, b_f, conv_w, conv_b, conv_ln_g, conv_ln_b, g_attn_out, w_out, g_mix, g_mlp, w_up, w_down,
           g_ple, w_ple, w_ple_gate, g_final):
    depth = w_in.shape[0]
    nb, seq, _ = x_prompt.shape
    db, dseq, _ = x_sample.shape
    mp, ms = nb * seq, db * dseq
    keep = CONV_K - 1
    xp = x_prompt.reshape(mp, D_MODEL)
    xs = x_sample.reshape(ms, D_MODEL)
    gf = g_final.reshape(1, D_MODEL)
    w = _stacked_weights(w_in, b_f, conv_w, conv_b, conv_ln_g, conv_ln_b, g_attn_out, w_out, g_mix, g_mlp,
                         w_up, w_down, g_ple, w_ple, w_ple_gate)
    pe_prompt = p_prompt.reshape(depth, mp, -1)
    pe_sample = p_sample.reshape(depth, ms, -1)
    cache_lf = jnp.swapaxes(cache_logf, 2, 3)
    cache_k = cache_k.reshape(cache_k.shape[0], cache_k.shape[1], -1, HEAD_DIM)
    cache_v = cache_v.reshape(cache_v.shape[0], cache_v.shape[1], -1, HEAD_DIM)
    zero_state = jnp.zeros((nb, CONV_HALO, CONV_WIDTH), f32)
    heads_first = lambda a: a.reshape(db, dseq, N_HEADS, -1).transpose(0, 2, 1, 3)
    pad_tok = lambda a, n: jnp.pad(a, ((0, 0), (0, 0), (0, n - dseq), (0, 0)))
    kv_prompt = None
    outs = {name: [] for name in ("lfp", "cvp", "ks", "vs", "lfs", "cvs")}
    for l in range(depth):
        final = l == depth - 1

        q, k_all, v_all, kb, vt, lf, crep, u = _in_proj(l, xp, w["g_mix"], w, kv_prompt, tm=512, seg_len=seq,
                                                        prompt=True)
        kv_prompt = (k_all, v_all)
        attn = _prompt_attn(l, q, kb, vt, crep, w["g_rep"], batch=nb, seq=seq, tq=1024, hp=2)
        u3 = u.reshape(nb, seq, CONV_WIDTH)
        cy = _conv(l, u3, zero_state, w, tl=512, rc=32).reshape(mp, CONV_WIDTH)
        xp = _finish(l, xp, attn, cy, pe_prompt, w, gf, tm=512, tf=1024, final=final)
        outs["lfp"].append(lf.reshape(N_HEADS, nb, seq).transpose(1, 2, 0))
        outs["cvp"].append(u3[:, seq - keep:, :])

        q, k, v, lf, u = _in_proj(l, xs, w["g_mix"], w, None, tm=ms, seg_len=dseq, prompt=False)
        lf = lf[:, :N_HEADS].reshape(db, dseq, N_HEADS)
        attn = _decode_attn(
            l, page_table, pad_tok(heads_first(q), BF16_ROWS),
            pad_tok(heads_first(k), LANES).astype(bf16), pad_tok(heads_first(v), LANES).astype(bf16),
            jnp.pad(lf.transpose(0, 2, 1), ((0, 0), (0, 0), (0, LANES - dseq))),
            w["g_heads"], cache_k, cache_v, cache_lf, pages_per_step=16, n_tok=dseq)
        attn = attn[:, :, :dseq, :].transpose(0, 2, 1, 3).reshape(ms, ATT_WIDTH)
        u3 = u.reshape(db, dseq, CONV_WIDTH)
        state = jnp.pad(state_conv[l], ((0, 0), (CONV_HALO - keep, 0), (0, 0)))
        cy = _conv(l, u3, state, w, tl=dseq, rc=dseq).reshape(ms, CONV_WIDTH)
        xs = _finish(l, xs, attn, cy, pe_sample, w, gf, tm=ms, tf=1024, final=final)
        outs["ks"].append(k.reshape(db, dseq, N_HEADS, HEAD_DIM))
        outs["vs"].append(v.reshape(db, dseq, N_HEADS, HEAD_DIM))
        outs["lfs"].append(lf)
        outs["cvs"].append(jnp.concatenate([state_conv[l], u3], axis=1)[:, -keep:, :])
    st = lambda name: jnp.stack(outs[name])
    k_all, v_all = kv_prompt
    return (xp.reshape(nb, seq, D_MODEL), xs.reshape(db, dseq, D_MODEL),
            k_all.reshape(depth, nb, seq, N_HEADS, HEAD_DIM), v_all.reshape(depth, nb, seq, N_HEADS, HEAD_DIM),
            st("lfp"), st("cvp"), st("ks"), st("vs"), st("lfs"), st("cvs"))
```
